```python
import math
import jax
import jax.numpy as jnp
from jax import lax
import numpy as np

D_MODEL = 1024
BATCH = 2
SEQ = 8192
DEPTH = 2
DEC_BATCH = 128
DEC_SEQ = 8
PAST_LEN = 16384
PAGE_SIZE = 128

DIFF_HEADS = 4
DIFF_KV_HEADS = 2
DIFF_GROUP = DIFF_HEADS // DIFF_KV_HEADS
DIFF_HD = 64
DIFF_VD = 2 * DIFF_HD
DIFF_ROT = DIFF_HD // 4
ROPE_THETA = 500000.0
MLA_HEADS = 4
Q_LORA = 256
KV_LORA = 128
MLA_NOPE = 64
MLA_ROPE = 32
MLA_QK = MLA_NOPE + MLA_ROPE
MLA_VD = 128
MLA_THETA = 10000.0
DIFF_WIDTH = DIFF_HEADS * DIFF_VD
MLA_WIDTH = MLA_HEADS * MLA_VD
MIX_WIDTH = DIFF_WIDTH + MLA_WIDTH
N_DQ = DIFF_HEADS * 2 * DIFF_HD
N_DK = DIFF_KV_HEADS * 2 * DIFF_HD
N_DV = DIFF_KV_HEADS * DIFF_VD
IN_COLS = N_DQ + N_DK + N_DV + Q_LORA + KV_LORA + MLA_ROPE
MEM_LEN = 256
CROSS_HEADS = 4
CROSS_HD = 64
CROSS_WIDTH = CROSS_HEADS * CROSS_HD
D_FF = 4 * D_MODEL
Q_BLOCK = 128
EPS = 1e-6

DIFF_S_EQ = "bqhgcd,bkhcd->bhgcqk"
DIFF_PV_EQ = "bhgcqk,bkhd->bhgcqd"
MLA_S_EQ = "bqhd,bkhd->bhqk"
MLA_PV_EQ = "bhqk,bkhd->bhqd"

kernel_name = "hybrid_diffattn_mla_memory_decoder_step"


def _rmsnorm(x, g):
    x32 = x.astype(jnp.float32)
    y = x32 * lax.rsqrt(jnp.mean(x32 * x32, axis=-1, keepdims=True) + EPS)
    return (y * g.astype(jnp.float32)).astype(x.dtype)


def _rope(x, pos, theta):
    d = x.shape[-1]
    inv = theta ** (-jnp.arange(0, d, 2, dtype=jnp.float32) / d)
    ang = pos.astype(jnp.float32)[:, None] * inv[None, :]
    ang = ang.reshape((ang.shape[0],) + (1,) * (x.ndim - 3) + (d // 2,))
    cos, sin = jnp.cos(ang), jnp.sin(ang)
    x32 = x.astype(jnp.float32)
    x1, x2 = x32[..., : d // 2], x32[..., d // 2:]
    return jnp.concatenate([x1 * cos - x2 * sin, x2 * cos + x1 * sin], axis=-1).astype(x.dtype)


def _partial_rope(x, pos, n_rot, theta):
    return jnp.concatenate([_rope(x[..., :n_rot], pos, theta), x[..., n_rot:]], axis=-1)


def _diff_lambda(lam, layer):
    lam_init = 0.8 - 0.6 * math.exp(-0.3 * layer)
    l32 = lam.astype(jnp.float32)
    lam_full = jnp.exp(jnp.sum(l32[0] * l32[1])) - jnp.exp(jnp.sum(l32[2] * l32[3])) + lam_init
    return lam_full, lam_init


def _mixer_inputs(h, pos, lp):
    b, s, _ = h.shape
    z = h @ lp["w_in"]
    c1 = N_DQ
    c2 = c1 + N_DK
    c3 = c2 + N_DV
    c4 = c3 + Q_LORA
    c5 = c4 + KV_LORA
    dq = z[..., :c1].reshape(b, s, DIFF_KV_HEADS, DIFF_GROUP, 2, DIFF_HD)
    dk = z[..., c1:c2].reshape(b, s, DIFF_KV_HEADS, 2, DIFF_HD)
    dv = z[..., c2:c3].reshape(b, s, DIFF_KV_HEADS, DIFF_VD)
    cq, ckv, krope = z[..., c3:c4], z[..., c4:c5], z[..., c5:]
    dq = _partial_rope(_rmsnorm(dq, lp["diff_q_norm"]), pos, DIFF_ROT, ROPE_THETA)
    dk = _partial_rope(_rmsnorm(dk, lp["diff_k_norm"]), pos, DIFF_ROT, ROPE_THETA)
    q = (_rmsnorm(cq, lp["mla_q_lora_norm"]) @ lp["w_uq"]).reshape(b, s, MLA_HEADS, MLA_QK)
    q = _rmsnorm(q, lp["mla_q_norm"])
    q = jnp.concatenate([q[..., :MLA_NOPE], _rope(q[..., MLA_NOPE:], pos, MLA_THETA)], axis=-1)
    ckv = _rmsnorm(ckv, lp["mla_kv_lora_norm"])
    return dq, dk, dv, q, ckv, krope


def _mla_keys(ckv, krope, pos, lp):
    b, s, _ = ckv.shape
    kv = (ckv @ lp["w_ukv"]).reshape(b, s, MLA_HEADS, MLA_NOPE + MLA_VD)
    k_nope, v = kv[..., :MLA_NOPE], kv[..., MLA_NOPE:]
    k_pe = jnp.broadcast_to(krope[:, :, None, :], (b, s, MLA_HEADS, MLA_ROPE)).astype(k_nope.dtype)
    k = _rmsnorm(jnp.concatenate([k_nope, k_pe], axis=-1), lp["mla_k_norm"])
    k = jnp.concatenate([k[..., :MLA_NOPE], _rope(k[..., MLA_NOPE:], pos, MLA_THETA)], axis=-1)
    return k, v


def _prompt_causal(q, k, v, s_eq, finish):
    b, s = q.shape[:2]
    nb = s // Q_BLOCK
    qb = jnp.moveaxis(q.reshape((b, nb, Q_BLOCK) + q.shape[2:]), 1, 0)
    kpos = jnp.arange(s)

    def block(args):
        qi, i = args
        sc = jnp.einsum(s_eq, qi.astype(jnp.float32), k.astype(jnp.float32))
        qpos = i * Q_BLOCK + jnp.arange(Q_BLOCK)
        sc = jnp.where(kpos[None, :] <= qpos[:, None], sc, -jnp.inf)
        return finish(jax.nn.softmax(sc, axis=-1), v)

    out = jnp.moveaxis(lax.map(block, (qb, jnp.arange(nb))), 0, 1)
    return out.reshape((b, s) + out.shape[3:])


def _online_step(carry, sc, v, pv_eq):
    m, l, acc = carry
    m_new = jnp.maximum(m, jnp.max(sc, axis=-1))
    corr = jnp.exp(m - m_new)
    p = jnp.exp(sc - m_new[..., None])
    return (m_new, l * corr + jnp.sum(p, axis=-1),
            acc * corr[..., None] + jnp.einsum(pv_eq, p, v.astype(jnp.float32)))


def _sample_attention(q, k_new, v_new, page_kv, page_table, s_eq, pv_eq):
    t = q.shape[1]
    q = q.astype(jnp.float32)
    sc = jnp.einsum(s_eq, q, k_new.astype(jnp.float32))
    tri = jnp.arange(t)[None, :] <= jnp.arange(t)[:, None]
    sc = jnp.where(tri, sc, -jnp.inf)
    m = jnp.max(sc, axis=-1)
    p = jnp.exp(sc - m[..., None])
    carry = (m, jnp.sum(p, axis=-1), jnp.einsum(pv_eq, p, v_new.astype(jnp.float32)))

    def body(c, xs):
        j, phys = xs
        kp, vp = page_kv(j, phys)
        return _online_step(c, jnp.einsum(s_eq, q, kp.astype(jnp.float32)), vp, pv_eq), None

    n_pages = page_table.shape[1]
    (m, l, acc), _ = lax.scan(body, carry, (jnp.arange(n_pages), page_table.T))
    return acc / l[..., None]


def _merge_heads(od, om, lam_init, lp):
    b, s = od.shape[:2]
    od = _rmsnorm(od, lp["diff_subln"]) * (1.0 - lam_init)
    om = _rmsnorm(om, lp["mla_out_norm"])
    o = jnp.concatenate([od.reshape(b, s, DIFF_WIDTH), om.reshape(b, s, MLA_WIDTH)], axis=-1)
    return o.astype(lp["w_o"].dtype) @ lp["w_o"]


def _memory_kv(mem, lp):
    b, m, _ = mem.shape
    hm = _rmsnorm(mem, lp["norm_mem"])
    k = _rmsnorm((hm @ lp["w_ck"]).reshape(b, m, CROSS_HEADS, CROSS_HD), lp["cross_k_norm"])
    v = (hm @ lp["w_cv"]).reshape(b, m, CROSS_HEADS, CROSS_HD)
    return k, v


def _cross_attention(h, mem_k, mem_v, lp):
    b, s, _ = h.shape
    q = _rmsnorm((h @ lp["w_cq"]).reshape(b, s, CROSS_HEADS, CROSS_HD), lp["cross_q_norm"])
    sc = jnp.einsum("bqhd,bkhd->bhqk", q.astype(jnp.float32) * CROSS_HD ** -0.5, mem_k.astype(jnp.float32))
    o = jnp.einsum("bhqk,bkhd->bqhd", jax.nn.softmax(sc, axis=-1), mem_v.astype(jnp.float32))
    return o.reshape(b, s, CROSS_WIDTH).astype(h.dtype) @ lp["w_co"]


def _sqrelu_mlp(h, lp):
    u = jax.nn.relu(h @ lp["w_up"])
    return (u * u) @ lp["w_down"]


def _diff_finish_fn(lam):
    def finish(p, v):
        return jnp.einsum("bhgqk,bkhd->bqhgd", p[:, :, :, 0] - lam * p[:, :, :, 1], v.astype(jnp.float32))
    return finish


def _mla_finish(p, v):
    return jnp.einsum("bhqk,bkhd->bqhd", p, v.astype(jnp.float32))


def _prompt_layer(x, mem, lp, layer):
    b, s, _ = x.shape
    pos = jnp.arange(s)
    lam, lam_init = _diff_lambda(lp["diff_lambda"], layer)
    dq, dk, dv, q, ckv, krope = _mixer_inputs(_rmsnorm(x, lp["norm_attn"]), pos, lp)
    od = _prompt_causal(dq.astype(jnp.float32) * DIFF_HD ** -0.5, dk, dv, DIFF_S_EQ, _diff_finish_fn(lam))
    mk, mv = _mla_keys(ckv, krope, pos, lp)
    om = _prompt_causal(q.astype(jnp.float32) * MLA_QK ** -0.5, mk, mv, MLA_S_EQ, _mla_finish)
    x = x + _merge_heads(od, om, lam_init, lp).astype(x.dtype)
    mem_k, mem_v = _memory_kv(mem, lp)
    x = x + _cross_attention(_rmsnorm(x, lp["norm_cross"]), mem_k, mem_v, lp)
    x = x + _sqrelu_mlp(_rmsnorm(x, lp["norm_mlp"]), lp)
    return x, (dk, dv, ckv, krope, mem_k, mem_v)


def _sample_layer(x, page_table, cache_diff_k, cache_diff_v, cache_mla_ckv, cache_mla_krope,
                  mem_k, mem_v, lp, layer):
    b, t, _ = x.shape
    page = cache_diff_k.shape[2]
    pos = page_table.shape[1] * page + jnp.arange(t)
    lam, lam_init = _diff_lambda(lp["diff_lambda"], layer)
    dq, dk, dv, q, ckv, krope = _mixer_inputs(_rmsnorm(x, lp["norm_attn"]), pos, lp)

    def diff_page(j, phys):
        return cache_diff_k[layer, phys], cache_diff_v[layer, phys]

    o = _sample_attention(dq.astype(jnp.float32) * DIFF_HD ** -0.5, dk, dv, diff_page, page_table,
                          DIFF_S_EQ, DIFF_PV_EQ)
    od = jnp.moveaxis(o[:, :, :, 0] - lam * o[:, :, :, 1], 3, 1)

    def mla_page(j, phys):
        ppos = j * page + jnp.arange(page)
        return _mla_keys(cache_mla_ckv[layer, phys], cache_mla_krope[layer, phys], ppos, lp)

    mk, mv = _mla_keys(ckv, krope, pos, lp)
    o = _sample_attention(q.astype(jnp.float32) * MLA_QK ** -0.5, mk, mv, mla_page, page_table,
                          MLA_S_EQ, MLA_PV_EQ)
    om = jnp.moveaxis(o, 2, 1)
    x = x + _merge_heads(od, om, lam_init, lp).astype(x.dtype)
    x = x + _cross_attention(_rmsnorm(x, lp["norm_cross"]), mem_k, mem_v, lp)
    x = x + _sqrelu_mlp(_rmsnorm(x, lp["norm_mlp"]), lp)
    return x, (dk, dv, ckv, krope)


def setup_inputs(seed: int = 0) -> dict:
    key = jax.random.key(seed)
    ks = iter(jax.random.split(key, 48))

    def nrm(shape, scale=1.0):
        return jax.random.normal(next(ks), shape, jnp.float32) * scale

    def gain(shape):
        return 1.0 + 0.02 * jax.random.normal(next(ks), shape, jnp.float32)

    n_pages = PAST_LEN // PAGE_SIZE
    n_used = DEC_BATCH * n_pages
    n_pool = n_used + (n_used + 3) // 4
    page_table = jax.random.permutation(next(ks), n_pool)[:n_used].reshape(DEC_BATCH, n_pages).astype(jnp.int32)
    return {
        "x_prompt": nrm((BATCH, SEQ, D_MODEL)),
        "x_sample": nrm((DEC_BATCH, DEC_SEQ, D_MODEL)),
        "cache_diff_k": nrm((DEPTH, n_pool, PAGE_SIZE, DIFF_KV_HEADS, 2, DIFF_HD)),
        "cache_diff_v": nrm((DEPTH, n_pool, PAGE_SIZE, DIFF_KV_HEADS, DIFF_VD)),
        "cache_mla_ckv": nrm((DEPTH, n_pool, PAGE_SIZE, KV_LORA)),
        "cache_mla_krope": nrm((DEPTH, n_pool, PAGE_SIZE, MLA_ROPE)),
        "cache_mem_k": nrm((DEPTH, DEC_BATCH, MEM_LEN, CROSS_HEADS, CROSS_HD)),
        "cache_mem_v": nrm((DEPTH, DEC_BATCH, MEM_LEN, CROSS_HEADS, CROSS_HD)),
        "page_table": page_table,
        "mem_prompt": nrm((BATCH, MEM_LEN, D_MODEL)),
        "norm_attn": gain((DEPTH, D_MODEL)),
        "w_in": nrm((DEPTH, D_MODEL, IN_COLS), D_MODEL ** -0.5),
        "diff_q_norm": gain((DEPTH, DIFF_HD)),
        "diff_k_norm": gain((DEPTH, DIFF_HD)),
        "diff_lambda": nrm((DEPTH, 4, DIFF_HD), 0.1),
        "diff_subln": gain((DEPTH, DIFF_VD)),
        "mla_q_lora_norm": gain((DEPTH, Q_LORA)),
        "w_uq": nrm((DEPTH, Q_LORA, MLA_HEADS * MLA_QK), Q_LORA ** -0.5),
        "mla_kv_lora_norm": gain((DEPTH, KV_LORA)),
        "w_ukv": nrm((DEPTH, KV_LORA, MLA_HEADS * (MLA_NOPE + MLA_VD)), KV_LORA ** -0.5),
        "mla_q_norm": gain((DEPTH, MLA_QK)),
        "mla_k_norm": gain((DEPTH, MLA_QK)),
        "mla_out_norm": gain((DEPTH, MLA_VD)),
        "w_o": nrm((DEPTH, MIX_WIDTH, D_MODEL), MIX_WIDTH ** -0.5),
        "norm_cross": gain((DEPTH, D_MODEL)),
        "norm_mem": gain((DEPTH, D_MODEL)),
        "w_cq": nrm((DEPTH, D_MODEL, CROSS_WIDTH), D_MODEL ** -0.5),
        "w_ck": nrm((DEPTH, D_MODEL, CROSS_WIDTH), D_MODEL ** -0.5),
        "w_cv": nrm((DEPTH, D_MODEL, CROSS_WIDTH), D_MODEL ** -0.5),
        "cross_q_norm": gain((DEPTH, CROSS_HD)),
        "cross_k_norm": gain((DEPTH, CROSS_HD)),
        "w_co": nrm((DEPTH, CROSS_WIDTH, D_MODEL), CROSS_WIDTH ** -0.5),
        "norm_mlp": gain((DEPTH, D_MODEL)),
        "w_up": nrm((DEPTH, D_MODEL, D_FF), D_MODEL ** -0.5),
        "w_down": nrm((DEPTH, D_FF, D_MODEL), D_FF ** -0.5),
    }


def reference(x_prompt, x_sample, cache_diff_k, cache_diff_v, cache_mla_ckv, cache_mla_krope,
              cache_mem_k, cache_mem_v, page_table, mem_prompt,
              norm_attn, w_in, diff_q_norm, diff_k_norm, diff_lambda, diff_subln,
              mla_q_lora_norm, w_uq, mla_kv_lora_norm, w_ukv, mla_q_norm, mla_k_norm, mla_out_norm,
              w_o, norm_cross, norm_mem, w_cq, w_ck, w_cv, cross_q_norm, cross_k_norm, w_co,
              norm_mlp, w_up, w_down):
    yp, ys = x_prompt, x_sample
    p_rows, s_rows = [], []
    for layer in range(DEPTH):
        lp = {
            "norm_attn": norm_attn[layer], "w_in": w_in[layer],
            "diff_q_norm": diff_q_norm[layer], "diff_k_norm": diff_k_norm[layer],
            "diff_lambda": diff_lambda[layer], "diff_subln": diff_subln[layer],
            "mla_q_lora_norm": mla_q_lora_norm[layer], "w_uq": w_uq[layer],
            "mla_kv_lora_norm": mla_kv_lora_norm[layer], "w_ukv": w_ukv[layer],
            "mla_q_norm": mla_q_norm[layer], "mla_k_norm": mla_k_norm[layer],
            "mla_out_norm": mla_out_norm[layer], "w_o": w_o[layer],
            "norm_cross": norm_cross[layer], "norm_mem": norm_mem[layer],
            "w_cq": w_cq[layer], "w_ck": w_ck[layer], "w_cv": w_cv[layer],
            "cross_q_norm": cross_q_norm[layer], "cross_k_norm": cross_k_norm[layer],
            "w_co": w_co[layer], "norm_mlp": norm_mlp[layer],
            "w_up": w_up[layer], "w_down": w_down[layer],
        }
        yp, pr = _prompt_layer(yp, mem_prompt, lp, layer)
        ys, sr = _sample_layer(ys, page_table, cache_diff_k, cache_diff_v, cache_mla_ckv, cache_mla_krope,
                               cache_mem_k[layer], cache_mem_v[layer], lp, layer)
        p_rows.append(pr)
        s_rows.append(sr)
    p_diff_k = jnp.stack([r[0] for r in p_rows])
    p_diff_v = jnp.stack([r[1] for r in p_rows])
    p_mla_ckv = jnp.stack([r[2] for r in p_rows])
    p_mla_krope = jnp.stack([r[3] for r in p_rows])
    p_mem_k = jnp.stack([r[4] for r in p_rows])
    p_mem_v = jnp.stack([r[5] for r in p_rows])
    s_diff_k = jnp.stack([r[0] for r in s_rows])
    s_diff_v = jnp.stack([r[1] for r in s_rows])
    s_mla_ckv = jnp.stack([r[2] for r in s_rows])
    s_mla_krope = jnp.stack([r[3] for r in s_rows])
    return (yp, ys, p_diff_k, p_diff_v, p_mla_ckv, p_mla_krope, p_mem_k, p_mem_v,
            s_diff_k, s_diff_v, s_mla_ckv, s_mla_krope)
```

```python
import functools
import math

import jax
import jax.numpy as jnp
from jax import lax
from jax.experimental import pallas as pl
from jax.experimental.pallas import tpu as pltpu

F32 = jnp.float32
BF16 = jnp.bfloat16

LANES = 128
D_MODEL = 1024
PAGE = 128
DIFF_KV_HEADS = 2
DIFF_GROUP = 2
DIFF_HD = 64
DIFF_VD = 128
DIFF_ROT = 16
ROPE_THETA = 500000.0
MLA_HEADS = 4
Q_LORA = 256
KV_LORA = 128
MLA_NOPE = 64
MLA_ROPE = 32
MLA_QK = MLA_NOPE + MLA_ROPE
MLA_VD = 128
MLA_THETA = 10000.0
N_DQ = 512
N_DK = 256
N_DV = 256
IN_COLS = N_DQ + N_DK + N_DV + Q_LORA + KV_LORA + MLA_ROPE
IN_COLS_PAD = 1536
CROSS_HEADS = 4
CROSS_HD = 64
CROSS_WIDTH = 256
D_FF = 4096
EPS = 1e-6
NEG = -1e30
VMEM_LIMIT = 48 * 1024 * 1024


def _cparams(sem):
    return pltpu.CompilerParams(dimension_semantics=sem, vmem_limit_bytes=VMEM_LIMIT)


def _rms(x, g):
    return x * lax.rsqrt(jnp.mean(x * x, axis=-1, keepdims=True) + EPS) * g


def _seg_norm(v, bmat, n, g):
    ssq = jnp.dot((v * v).astype(BF16), bmat, preferred_element_type=F32)
    return v * lax.rsqrt(ssq * (1.0 / n) + EPS) * g


def _rope_groups(v, c, s1, s2, sh):
    outs = []
    for j in range(v.shape[1] // LANES):
        g = v[:, j * LANES:(j + 1) * LANES]
        outs.append(g * c + pltpu.roll(g, LANES - sh, 1) * s1 + pltpu.roll(g, sh, 1) * s2)
    return jnp.concatenate(outs, axis=1)


def _dot(a, b):
    return jnp.dot(a, b, preferred_element_type=F32)


def _dot_nt(a, b):
    return lax.dot_general(a, b, (((1,), (1,)), ((), ())), preferred_element_type=F32)


def _mixer_kernel(x_ref, ga_ref, win_ref, gq_ref, gk_ref, b64_ref, dc_ref, ds1_ref, ds2_ref,
                  gql_ref, wuq_ref, gmq_ref, b128_ref, mc_ref, ms1_ref, ms2_ref,
                  gkv_ref, wuk_ref, wuv_ref, gmk_ref,
                  qd_ref, dk_ref, dv_ref, kdb_ref, vdb_ref, qm_ref, ckv_ref, kr_ref, km_ref, vm_ref):
    tm = x_ref.shape[0]
    h = _rms(x_ref[...], ga_ref[...]).astype(BF16)
    z = _dot(h, win_ref[...])
    dq, dk, dv = z[:, 0:512], z[:, 512:768], z[:, 768:1024]
    cq, ckv, krg = z[:, 1024:1280], z[:, 1280:1408], z[:, 1408:1536]
    dc, ds1, ds2 = dc_ref[...], ds1_ref[...], ds2_ref[...]
    b64 = b64_ref[...]

    dqn = _rope_groups(_seg_norm(dq, b64, DIFF_HD, gq_ref[...]), dc, ds1, ds2, DIFF_ROT // 2)
    dqn = dqn * (DIFF_HD ** -0.5)
    lane = lax.broadcasted_iota(jnp.int32, (tm, LANES), 1)
    for j in range(4):
        grp = dqn[:, j * LANES:(j + 1) * LANES]
        qd_ref[:, (2 * j) * LANES:(2 * j + 1) * LANES] = jnp.where(lane < DIFF_HD, grp, 0.0).astype(BF16)
        qd_ref[:, (2 * j + 1) * LANES:(2 * j + 2) * LANES] = jnp.where(lane >= DIFF_HD, grp, 0.0).astype(BF16)

    dkn = _rope_groups(_seg_norm(dk, b64[0:256, 0:256], DIFF_HD, gk_ref[...]), dc, ds1, ds2, DIFF_ROT // 2)
    dk_ref[...] = dkn
    kdb_ref[...] = dkn.astype(BF16)
    dv_ref[...] = dv
    vdb_ref[...] = dv.astype(BF16)

    mc, ms1, ms2 = mc_ref[...], ms1_ref[...], ms2_ref[...]
    b128 = b128_ref[...]
    q = _dot(_rms(cq, gql_ref[...]).astype(BF16), wuq_ref[...])
    qn = _rope_groups(_seg_norm(q, b128, MLA_QK, gmq_ref[...]), mc, ms1, ms2, MLA_ROPE // 2)
    qm_ref[...] = (qn * (MLA_QK ** -0.5)).astype(BF16)

    ckvn = _rms(ckv, gkv_ref[...])
    ckv_ref[...] = ckvn
    kr_ref[...] = krg[:, 0:MLA_ROPE]
    cb = ckvn.astype(BF16)
    krs = pltpu.roll(krg, MLA_NOPE, 1)
    kfull = _dot(cb, wuk_ref[...]) + jnp.concatenate([krs] * MLA_HEADS, axis=1)
    kn = _rope_groups(_seg_norm(kfull, b128, MLA_QK, gmk_ref[...]), mc, ms1, ms2, MLA_ROPE // 2)
    km_ref[...] = kn.astype(BF16)
    vm_ref[...] = _dot(cb, wuv_ref[...]).astype(BF16)


def _mixer_call(x, lw, tabs, n_tab_blocks, tm):
    t = x.shape[0]
    nt = t // tm
    tok = lambda w: pl.BlockSpec((tm, w), lambda i: (i, 0))
    full = lambda a: pl.BlockSpec(a.shape, lambda i: (0,) * a.ndim)
    tab = pl.BlockSpec((tm, LANES), lambda i: (i % n_tab_blocks, 0))
    ins = [x, lw["g_attn"], lw["w_in"], lw["g_dq"], lw["g_dk"], lw["b64"], tabs["dc"], tabs["ds1"], tabs["ds2"],
           lw["g_ql"], lw["w_uq"], lw["g_mq"], lw["b128"], tabs["mc"], tabs["ms1"], tabs["ms2"],
           lw["g_kvl"], lw["w_uk"], lw["w_uv"], lw["g_mk"]]
    in_specs = [tok(D_MODEL)] + [full(a) for a in ins[1:6]] + [tab] * 3 + [full(a) for a in ins[9:13]] \
        + [tab] * 3 + [full(a) for a in ins[16:20]]
    outs = [(1024, BF16), (256, F32), (256, F32), (256, BF16), (256, BF16), (512, BF16), (128, F32),
            (MLA_ROPE, F32), (512, BF16), (512, BF16)]
    return pl.pallas_call(
        _mixer_kernel,
        grid=(nt,),
        in_specs=in_specs,
        out_specs=[tok(w) for w, _ in outs],
        out_shape=[jax.ShapeDtypeStruct((t, w), d) for w, d in outs],
        compiler_params=_cparams(("parallel",)),
        name="mixer_inputs",
    )(*ins)


def _flash_kernel(qa_ref, ka_ref, fl_ref, q_ref, k_ref, v_ref, o_ref, qs, m_s, l_s, acc, *, groups, tq, tk):
    step = pl.program_id(2)
    qi, ki, fl = qa_ref[step], ka_ref[step], fl_ref[step]

    @pl.when(ki == 0)
    def _():
        for g in range(groups):
            qs[g * tq:(g + 1) * tq, :] = q_ref[:, g * LANES:(g + 1) * LANES]
        m_s[...] = jnp.full(m_s.shape, NEG, F32)
        l_s[...] = jnp.zeros(l_s.shape, F32)
        acc[...] = jnp.zeros(acc.shape, F32)

    def update(masked):
        s = _dot_nt(qs[...], k_ref[...])
        if masked:
            row = lax.broadcasted_iota(jnp.int32, (tq, tk), 0) + qi * tq
            col = lax.broadcasted_iota(jnp.int32, (tq, tk), 1) + ki * tk
            keep = (col <= row)[None]
            s = jnp.where(keep, s.reshape(groups, tq, tk), NEG).reshape(groups * tq, tk)
        m_prev = m_s[...]
        m_new = jnp.maximum(m_prev, jnp.max(s, axis=1, keepdims=True))
        alpha = jnp.exp(m_prev - m_new)
        p = jnp.exp(s - m_new)
        l_s[...] = alpha * l_s[...] + jnp.sum(p, axis=1, keepdims=True)
        acc[...] = alpha * acc[...] + _dot(p.astype(BF16), v_ref[...])
        m_s[...] = m_new

    pl.when((fl & 2) != 0)(lambda: update(True))
    pl.when((fl & 2) == 0)(lambda: update(False))

    @pl.when((fl & 1) != 0)
    def _():
        o = acc[...] / l_s[...]
        for g in range(groups):
            o_ref[:, g * LANES:(g + 1) * LANES] = o[g * tq:(g + 1) * tq, :]


def _flash_call(q, k, v, *, groups, tq, tk, name):
    b, s, _ = q.shape
    hk = k.shape[2] // LANES
    qs_, ks_, fs_ = [], [], []
    for qi in range(s // tq):
        last = (qi * tq + tq - 1) // tk
        for ki in range(last + 1):
            diag = (ki * tk + tk - 1) > (qi * tq)
            qs_.append(qi)
            ks_.append(ki)
            fs_.append((1 if ki == last else 0) | (2 if diag else 0))
    nsteps = len(qs_)
    rows = groups * tq
    gs = pltpu.PrefetchScalarGridSpec(
        num_scalar_prefetch=3,
        grid=(b, hk, nsteps),
        in_specs=[pl.BlockSpec((None, tq, groups * LANES), lambda b_, h, t, qa, ka, fl: (b_, qa[t], h)),
                  pl.BlockSpec((None, tk, LANES), lambda b_, h, t, qa, ka, fl: (b_, ka[t], h)),
                  pl.BlockSpec((None, tk, LANES), lambda b_, h, t, qa, ka, fl: (b_, ka[t], h))],
        out_specs=pl.BlockSpec((None, tq, groups * LANES), lambda b_, h, t, qa, ka, fl: (b_, qa[t], h)),
        scratch_shapes=[pltpu.VMEM((rows, LANES), BF16), pltpu.VMEM((rows, 1), F32),
                        pltpu.VMEM((rows, 1), F32), pltpu.VMEM((rows, LANES), F32)],
    )
    return pl.pallas_call(
        functools.partial(_flash_kernel, groups=groups, tq=tq, tk=tk),
        grid_spec=gs,
        out_shape=jax.ShapeDtypeStruct(q.shape, F32),
        compiler_params=_cparams(("parallel", "parallel", "arbitrary")),
        name=name,
    )(jnp.array(qs_, jnp.int32), jnp.array(ks_, jnp.int32), jnp.array(fs_, jnp.int32), q, k, v)


def _merge_kernel(od_ref, om_ref, x_ref, lam_ref, gsub_ref, gmo_ref, wo_ref, y_ref, *, lam_init):
    lm = lam_ref[...]
    lam = (jnp.exp(jnp.sum(lm[0:1] * lm[1:2], axis=1, keepdims=True))
           - jnp.exp(jnp.sum(lm[2:3] * lm[3:4], axis=1, keepdims=True)) + lam_init)
    parts = []
    for j in range(4):
        o = od_ref[:, (2 * j) * LANES:(2 * j + 1) * LANES] - lam * od_ref[:, (2 * j + 1) * LANES:(2 * j + 2) * LANES]
        parts.append(_rms(o, gsub_ref[...]) * (1.0 - lam_init))
    for h in range(MLA_HEADS):
        parts.append(_rms(om_ref[:, h * LANES:(h + 1) * LANES], gmo_ref[...]))
    o = jnp.concatenate(parts, axis=1).astype(BF16)
    y_ref[...] = x_ref[...] + _dot(o, wo_ref[...])


def _merge_call(od, om, x, lw, layer, tm):
    t = x.shape[0]
    tok = lambda w: pl.BlockSpec((tm, w), lambda i: (i, 0))
    full = lambda a: pl.BlockSpec(a.shape, lambda i: (0,) * a.ndim)
    lam_init = 0.8 - 0.6 * math.exp(-0.3 * layer)
    ins = [od, om, x, lw["lam"], lw["g_sub"], lw["g_mo"], lw["w_o"]]
    return pl.pallas_call(
        functools.partial(_merge_kernel, lam_init=lam_init),
        grid=(t // tm,),
        in_specs=[tok(1024), tok(512), tok(D_MODEL)] + [full(a) for a in ins[3:]],
        out_specs=tok(D_MODEL),
        out_shape=jax.ShapeDtypeStruct((t, D_MODEL), F32),
        compiler_params=_cparams(("parallel",)),
        name="merge_out_proj",
    )(*ins)


def _memkv_kernel(mem_ref, gm_ref, wck_ref, wcv_ref, b64_ref, gk_ref, k_ref, v_ref):
    hm = _rms(mem_ref[...], gm_ref[...]).astype(BF16)
    k_ref[...] = _seg_norm(_dot(hm, wck_ref[...]), b64_ref[...], CROSS_HD, gk_ref[...])
    v_ref[...] = _dot(hm, wcv_ref[...])


def _memkv_call(mem, lw):
    t = mem.shape[0]
    tm = min(t, 256)
    tok = lambda w: pl.BlockSpec((tm, w), lambda i: (i, 0))
    full = lambda a: pl.BlockSpec(a.shape, lambda i: (0,) * a.ndim)
    ins = [mem, lw["g_mem"], lw["w_ck"], lw["w_cv"], lw["b64c"], lw["g_ck"]]
    return pl.pallas_call(
        _memkv_kernel,
        grid=(t // tm,),
        in_specs=[tok(D_MODEL)] + [full(a) for a in ins[1:]],
        out_specs=[tok(CROSS_WIDTH), tok(CROSS_WIDTH)],
        out_shape=[jax.ShapeDtypeStruct((t, CROSS_WIDTH), F32)] * 2,
        compiler_params=_cparams(("parallel",)),
        name="memory_kv",
    )(*ins)


def _cross_kernel(x_ref, gc_ref, wcq_ref, b64_ref, gq_ref, mk_ref, mv_ref, wco_ref, y_ref, *, nseq, tpm):
    x = x_ref[...]
    h = _rms(x, gc_ref[...]).astype(BF16)
    q = _seg_norm(_dot(h, wcq_ref[...]), b64_ref[...], CROSS_HD, gq_ref[...]) * (CROSS_HD ** -0.5)
    head_of_lane = lax.broadcasted_iota(jnp.int32, (tpm, CROSS_WIDTH), 1) // CROSS_HD
    outs = []
    for n in range(nseq):
        qn = q[n * tpm:(n + 1) * tpm, :]
        qbd = jnp.concatenate([jnp.where(head_of_lane == hh, qn, 0.0) for hh in range(CROSS_HEADS)], axis=0)
        s = _dot_nt(qbd.astype(BF16), mk_ref[n].astype(BF16))
        p = jnp.exp(s - jnp.max(s, axis=1, keepdims=True))
        p = p / jnp.sum(p, axis=1, keepdims=True)
        o = _dot(p.astype(BF16), mv_ref[n].astype(BF16))
        on = jnp.zeros((tpm, CROSS_WIDTH), F32)
        for hh in range(CROSS_HEADS):
            on = on + jnp.where(head_of_lane == hh, o[hh * tpm:(hh + 1) * tpm, :], 0.0)
        outs.append(on)
    o = jnp.concatenate(outs, axis=0) if nseq > 1 else outs[0]
    y_ref[...] = x + _dot(o.astype(BF16), wco_ref[...])


def _cross_call(x, mem_k, mem_v, lw, *, nseq, tpm, mem_map):
    t = x.shape[0]
    tm = nseq * tpm
    ml = mem_k.shape[1]
    tok = lambda w: pl.BlockSpec((tm, w), lambda i: (i, 0))
    full = lambda a: pl.BlockSpec(a.shape, lambda i: (0,) * a.ndim)
    mem = pl.BlockSpec((nseq, ml, CROSS_WIDTH), lambda i: (mem_map(i), 0, 0))
    ins = [x, lw["g_cross"], lw["w_cq"], lw["b64c"], lw["g_cq"], mem_k, mem_v, lw["w_co"]]
    return pl.pallas_call(
        functools.partial(_cross_kernel, nseq=nseq, tpm=tpm),
        grid=(t // tm,),
        in_specs=[tok(D_MODEL)] + [full(a) for a in ins[1:5]] + [mem, mem, full(ins[7])],
        out_specs=tok(D_MODEL),
        out_shape=jax.ShapeDtypeStruct((t, D_MODEL), F32),
        compiler_params=_cparams(("parallel",)),
        name="cross_attention",
    )(*ins)


def _mlp_kernel(x_ref, g_ref, wup_ref, wdn_ref, y_ref, h_s, acc):
    f = pl.program_id(1)

    @pl.when(f == 0)
    def _():
        h_s[...] = _rms(x_ref[...], g_ref[...]).astype(BF16)
        acc[...] = x_ref[...]

    u = jnp.maximum(_dot(h_s[...], wup_ref[...]), 0.0)
    acc[...] += _dot((u * u).astype(BF16), wdn_ref[...])

    @pl.when(f == pl.num_programs(1) - 1)
    def _():
        y_ref[...] = acc[...]


def _mlp_call(x, lw, tm, tf):
    t = x.shape[0]
    return pl.pallas_call(
        _mlp_kernel,
        grid=(t // tm, D_FF // tf),
        in_specs=[pl.BlockSpec((tm, D_MODEL), lambda i, f: (i, 0)),
                  pl.BlockSpec((1, D_MODEL), lambda i, f: (0, 0)),
                  pl.BlockSpec((D_MODEL, tf), lambda i, f: (0, f)),
                  pl.BlockSpec((tf, D_MODEL), lambda i, f: (f, 0))],
        out_specs=pl.BlockSpec((tm, D_MODEL), lambda i, f: (i, 0)),
        out_shape=jax.ShapeDtypeStruct((t, D_MODEL), F32),
        scratch_shapes=[pltpu.VMEM((tm, D_MODEL), BF16), pltpu.VMEM((tm, D_MODEL), F32)],
        compiler_params=_cparams(("parallel", "arbitrary")),
        name="sqrelu_mlp",
    )(x, lw["g_mlp"], lw["w_up"], lw["w_down"])


def _softmax_step(s, m_ref, l_ref, acc_ref, pv_fn, idx):
    m_prev = m_ref[idx][:, 0:1]
    m_new = jnp.maximum(m_prev, jnp.max(s, axis=1, keepdims=True))
    alpha = jnp.exp(m_prev - m_new)
    p = jnp.exp(s - m_new)
    l_new = alpha * l_ref[idx][:, 0:1] + jnp.sum(p, axis=1, keepdims=True)
    acc_ref[idx] = alpha * acc_ref[idx] + pv_fn(p.astype(BF16))
    m_ref[idx] = jnp.broadcast_to(m_new, m_ref.shape[1:])
    l_ref[idx] = jnp.broadcast_to(l_new, l_ref.shape[1:])


def _diff_paged_kernel(pt_ref, q_ref, kn_ref, vn_ref, *refs, pages, t_new):
    k_refs, v_refs = refs[:pages], refs[pages:2 * pages]
    o_ref, m_s, l_s, acc = refs[2 * pages:]
    c = pl.program_id(1)
    q = q_ref[...]
    rows = q.shape[0]

    @pl.when(c == 0)
    def _():
        pad = jnp.zeros((PAGE - t_new, 2 * LANES), F32)
        kn = jnp.concatenate([kn_ref[...], pad], axis=0).astype(BF16)
        vn = jnp.concatenate([vn_ref[...], pad], axis=0).astype(BF16)
        s = _dot_nt(q, kn)
        tq_ = lax.broadcasted_iota(jnp.int32, s.shape, 0) % t_new
        col = lax.broadcasted_iota(jnp.int32, s.shape, 1)
        s = jnp.where(col <= tq_, s, NEG)
        m = jnp.max(s, axis=1, keepdims=True)
        p = jnp.exp(s - m)
        m_s[0] = jnp.broadcast_to(m, (rows, LANES))
        l_s[0] = jnp.broadcast_to(jnp.sum(p, axis=1, keepdims=True), (rows, LANES))
        acc[0] = _dot(p.astype(BF16), vn)

    def k_page(r):
        return jnp.concatenate([r[pl.ds(j, PAGE, stride=4), :] for j in range(4)], axis=1).astype(BF16)

    def v_page(r):
        return jnp.concatenate([r[pl.ds(j, PAGE, stride=2), :] for j in range(2)], axis=1).astype(BF16)

    s = jnp.concatenate([_dot_nt(q, k_page(r)) for r in k_refs], axis=1)

    def pv(p):
        out = _dot(p[:, 0:PAGE], v_page(v_refs[0]))
        for i in range(1, pages):
            out = out + _dot(p[:, i * PAGE:(i + 1) * PAGE], v_page(v_refs[i]))
        return out

    _softmax_step(s, m_s, l_s, acc, pv, 0)

    @pl.when(c == pl.num_programs(1) - 1)
    def _():
        o = acc[0] / l_s[0][:, 0:1]
        half = rows // 2
        o_ref[...] = jnp.concatenate([o[0:half, 0:LANES], o[half:rows, LANES:2 * LANES]], axis=0)


def _diff_paged_call(layer, pt, qbd, k_new, v_new, cache_k, cache_v, *, pages):
    b, n_pages = pt.shape
    rows = qbd.shape[1]
    t_new = k_new.shape[1]

    def page_map(b_, c, pt_, *, i):
        return (layer, pt_[b_, c * pages + i], 0, 0)

    seq = lambda shp: pl.BlockSpec((None,) + shp, lambda b_, c, pt_: (b_, 0, 0))
    gs = pltpu.PrefetchScalarGridSpec(
        num_scalar_prefetch=1,
        grid=(b, n_pages // pages),
        in_specs=[seq((rows, 2 * LANES)), seq((t_new, 2 * LANES)), seq((t_new, 2 * LANES))]
        + [pl.BlockSpec((None, None, 4 * PAGE, DIFF_HD), functools.partial(page_map, i=i)) for i in range(pages)]
        + [pl.BlockSpec((None, None, 2 * PAGE, DIFF_VD), functools.partial(page_map, i=i)) for i in range(pages)],
        out_specs=seq((rows, LANES)),
        scratch_shapes=[pltpu.VMEM((1, rows, LANES), F32), pltpu.VMEM((1, rows, LANES), F32),
                        pltpu.VMEM((1, rows, 2 * LANES), F32)],
    )
    return pl.pallas_call(
        functools.partial(_diff_paged_kernel, pages=pages, t_new=t_new),
        grid_spec=gs,
        out_shape=jax.ShapeDtypeStruct((b, rows, LANES), F32),
        compiler_params=_cparams(("parallel", "arbitrary")),
        name="diff_paged_attention",
    )(pt, qbd, k_new, v_new, *([cache_k] * pages), *([cache_v] * pages))


def _mla_absorb_kernel(qm_ref, wabs_ref, lq_ref):
    for h in range(MLA_HEADS):
        lq_ref[:, h * 2 * LANES:(h + 1) * 2 * LANES] = _dot(
            qm_ref[:, h * LANES:(h + 1) * LANES], wabs_ref[h]).astype(BF16)


def _mla_absorb_call(qm, wabs):
    t = qm.shape[0]
    tm = min(t, 256)
    return pl.pallas_call(
        _mla_absorb_kernel,
        grid=(t // tm,),
        in_specs=[pl.BlockSpec((tm, MLA_HEADS * LANES), lambda i: (i, 0)),
                  pl.BlockSpec(wabs.shape, lambda i: (0, 0, 0))],
        out_specs=pl.BlockSpec((tm, MLA_HEADS * 2 * LANES), lambda i: (i, 0)),
        out_shape=jax.ShapeDtypeStruct((t, MLA_HEADS * 2 * LANES), BF16),
        compiler_params=_cparams(("parallel",)),
        name="mla_absorb_q",
    )(qm, wabs)


def _mla_paged_kernel(pt_ref, lq_ref, wrow_ref, cn_ref, rn_ref, tcn_ref, tsn_ref, tc_ref, ts_ref,
                      gr_ref, e1_ref, e2_ref, wuv_ref, *refs, pages, t_new):
    c_refs, r_refs = refs[:pages], refs[pages:2 * pages]
    o_ref, m_s, l_s, acc = refs[2 * pages:]
    c = pl.program_id(0)
    b = pl.program_id(1)
    lall = jnp.concatenate([lq_ref[...], wrow_ref[...]], axis=0)
    nq = lq_ref.shape[0]
    gr = gr_ref[...]
    lane = lax.broadcasted_iota(jnp.int32, (PAGE, LANES), 1)

    def features(ckv, kr, tcos, tsin):
        a = _dot((kr * gr).astype(BF16), e1_ref[...]) + _dot(kr.astype(BF16), e2_ref[...])
        lo, sw = a[:, 0:LANES], a[:, LANES:2 * LANES]
        g = lo * jnp.where(lane < MLA_ROPE, tcos, lo) + sw * tsin
        return jnp.concatenate([ckv.astype(BF16), g.astype(BF16)], axis=1)

    def scores(f):
        out = _dot_nt(lall, f)
        ss_rope = out[nq + 256:nq + 257, :]
        parts = []
        for h in range(MLA_HEADS):
            kn = out[nq + h * MLA_NOPE:nq + (h + 1) * MLA_NOPE, :]
            ss = jnp.sum(kn * kn, axis=0, keepdims=True) + ss_rope
            r = lax.rsqrt(ss * (1.0 / MLA_QK) + EPS)
            parts.append(out[h * t_new:(h + 1) * t_new, :] * r)
        return jnp.concatenate(parts, axis=0)

    @pl.when(c == 0)
    def _():
        padc = jnp.zeros((PAGE - t_new, KV_LORA), F32)
        padr = jnp.zeros((PAGE - t_new, MLA_ROPE), F32)
        f = features(jnp.concatenate([cn_ref[...], padc], axis=0),
                     jnp.concatenate([rn_ref[...], padr], axis=0), tcn_ref[...], tsn_ref[...])
        s = scores(f)
        tq_ = lax.broadcasted_iota(jnp.int32, s.shape, 0) % t_new
        col = lax.broadcasted_iota(jnp.int32, s.shape, 1)
        s = jnp.where(col <= tq_, s, NEG)
        m = jnp.max(s, axis=1, keepdims=True)
        p = jnp.exp(s - m)
        m_s[b] = jnp.broadcast_to(m, (nq, LANES))
        l_s[b] = jnp.broadcast_to(jnp.sum(p, axis=1, keepdims=True), (nq, LANES))
        acc[b] = _dot(p.astype(BF16), f[:, 0:LANES])

    fs = [features(c_refs[i][...], r_refs[i][...], tc_ref[i * PAGE:(i + 1) * PAGE, :],
                   ts_ref[i * PAGE:(i + 1) * PAGE, :]) for i in range(pages)]
    s = jnp.concatenate([scores(f) for f in fs], axis=1)

    def pv(p):
        out = _dot(p[:, 0:PAGE], fs[0][:, 0:LANES])
        for i in range(1, pages):
            out = out + _dot(p[:, i * PAGE:(i + 1) * PAGE], fs[i][:, 0:LANES])
        return out

    _softmax_step(s, m_s, l_s, acc, pv, b)

    @pl.when(c == pl.num_programs(0) - 1)
    def _():
        o = (acc[b] / l_s[b][:, 0:1]).astype(BF16)
        o_ref[b] = jnp.concatenate(
            [_dot(o[h * t_new:(h + 1) * t_new, :], wuv_ref[h]) for h in range(MLA_HEADS)], axis=0)


def _mla_paged_call(layer, pt, lq, ckv_new, kr_new, cache_ckv, cache_kr, lw, ptabs, *, pages):
    b, n_pages = pt.shape
    nq = lq.shape[1]
    t_new = ckv_new.shape[1]

    def page_map(c, b_, pt_, *, i):
        return (layer, pt_[b_, c * pages + i], 0, 0)

    seq = lambda shp: pl.BlockSpec((None,) + shp, lambda c, b_, pt_: (b_, 0, 0))
    full = lambda a: pl.BlockSpec(a.shape, lambda c, b_, pt_: (0,) * a.ndim)
    chunk = pl.BlockSpec((pages * PAGE, LANES), lambda c, b_, pt_: (c, 0))
    consts = [lw["g_kr"], lw["e1"], lw["e2"], lw["w_uvh"]]
    gs = pltpu.PrefetchScalarGridSpec(
        num_scalar_prefetch=1,
        grid=(n_pages // pages, b),
        in_specs=[seq((nq, 2 * LANES)), full(lw["w_rows"]), seq((t_new, KV_LORA)), seq((t_new, MLA_ROPE)),
                  full(ptabs["cn"]), full(ptabs["sn"]), chunk, chunk] + [full(a) for a in consts]
        + [pl.BlockSpec((None, None, PAGE, KV_LORA), functools.partial(page_map, i=i)) for i in range(pages)]
        + [pl.BlockSpec((None, None, PAGE, MLA_ROPE), functools.partial(page_map, i=i)) for i in range(pages)],
        out_specs=pl.BlockSpec((b, nq, LANES), lambda c, b_, pt_: (0, 0, 0)),
        scratch_shapes=[pltpu.VMEM((b, nq, LANES), F32), pltpu.VMEM((b, nq, LANES), F32),
                        pltpu.VMEM((b, nq, LANES), F32)],
    )
    return pl.pallas_call(
        functools.partial(_mla_paged_kernel, pages=pages, t_new=t_new),
        grid_spec=gs,
        out_shape=jax.ShapeDtypeStruct((b, nq, LANES), F32),
        compiler_params=_cparams(("arbitrary", "arbitrary")),
        name="mla_paged_attention",
    )(pt, lq, lw["w_rows"], ckv_new, kr_new, ptabs["cn"], ptabs["sn"], ptabs["c"], ptabs["s"],
      *consts, *([cache_ckv] * pages), *([cache_kr] * pages))


def _block_ones(n, seg):
    i = jnp.arange(n) // seg
    return (i[:, None] == i[None, :]).astype(BF16)


def _pad_heads(w, used):
    lead = w.shape[:-1]
    w = w.reshape(lead + (MLA_HEADS, used))
    w = jnp.pad(w, [(0, 0)] * len(lead) + [(0, 0), (0, LANES - used)])
    return w.reshape(lead + (MLA_HEADS * LANES,))


def _layer_weights(p, l):
    row = lambda a: a[l][None, :].astype(F32)
    w_ukv = p["w_ukv"][l].reshape(KV_LORA, MLA_HEADS, MLA_NOPE + MLA_VD)
    w_uk = w_ukv[:, :, :MLA_NOPE]
    w_uv = w_ukv[:, :, MLA_NOPE:]
    g_mk = p["mla_k_norm"][l]
    pad_g = lambda g: jnp.tile(jnp.pad(g, (0, LANES - MLA_QK)), MLA_HEADS)[None, :]
    wabs = jnp.zeros((MLA_HEADS, LANES, 2 * LANES), F32)
    wabs = wabs.at[:, :MLA_NOPE, :KV_LORA].set(jnp.transpose(w_uk, (1, 2, 0)) * g_mk[None, :MLA_NOPE, None])
    wabs = wabs.at[:, MLA_NOPE + jnp.arange(MLA_ROPE), KV_LORA + jnp.arange(MLA_ROPE)].set(1.0)
    w_rows = jnp.zeros((256 + 16, 2 * LANES), F32)
    w_rows = w_rows.at[:256, :KV_LORA].set(jnp.transpose(w_uk, (1, 2, 0)).reshape(256, KV_LORA))
    w_rows = w_rows.at[256, KV_LORA + MLA_ROPE:KV_LORA + 2 * MLA_ROPE].set(1.0)
    ar = jnp.arange(MLA_ROPE)
    half = MLA_ROPE // 2
    e1 = jnp.zeros((MLA_ROPE, 2 * LANES), F32)
    e1 = e1.at[ar, ar].set(1.0)
    e1 = e1.at[ar, LANES + (ar + half) % MLA_ROPE].set(1.0)
    e2 = jnp.zeros((MLA_ROPE, 2 * LANES), F32).at[ar, MLA_ROPE + ar].set(1.0)
    return {
        "g_attn": row(p["norm_attn"]),
        "w_in": jnp.pad(p["w_in"][l], ((0, 0), (0, IN_COLS_PAD - IN_COLS))).astype(BF16),
        "g_dq": jnp.tile(p["diff_q_norm"][l], 8)[None, :],
        "g_dk": jnp.tile(p["diff_k_norm"][l], 4)[None, :],
        "b64": _block_ones(512, 64),
        "b128": _block_ones(512, 128),
        "b64c": _block_ones(256, 64),
        "g_ql": row(p["mla_q_lora_norm"]),
        "w_uq": _pad_heads(p["w_uq"][l], MLA_QK).astype(BF16),
        "g_mq": pad_g(p["mla_q_norm"][l]),
        "g_kvl": row(p["mla_kv_lora_norm"]),
        "w_uk": _pad_heads(w_uk.reshape(KV_LORA, MLA_HEADS * MLA_NOPE), MLA_NOPE).astype(BF16),
        "w_uv": w_uv.reshape(KV_LORA, MLA_HEADS * MLA_VD).astype(BF16),
        "g_mk": pad_g(g_mk),
        "lam": p["diff_lambda"][l],
        "g_sub": row(p["diff_subln"]),
        "g_mo": row(p["mla_out_norm"]),
        "w_o": p["w_o"][l].astype(BF16),
        "g_mem": row(p["norm_mem"]),
        "w_ck": p["w_ck"][l].astype(BF16),
        "w_cv": p["w_cv"][l].astype(BF16),
        "g_ck": jnp.tile(p["cross_k_norm"][l], 4)[None, :],
        "g_cross": row(p["norm_cross"]),
        "w_cq": p["w_cq"][l].astype(BF16),
        "g_cq": jnp.tile(p["cross_q_norm"][l], 4)[None, :],
        "w_co": p["w_co"][l].astype(BF16),
        "g_mlp": row(p["norm_mlp"]),
        "w_up": p["w_up"][l].astype(BF16),
        "w_down": p["w_down"][l].astype(BF16),
        "wabs": wabs.astype(BF16),
        "w_rows": w_rows.astype(BF16),
        "g_kr": g_mk[None, MLA_NOPE:],
        "e1": e1.astype(BF16),
        "e2": e2.astype(BF16),
        "w_uvh": jnp.transpose(w_uv, (1, 0, 2)).astype(BF16),
    }


def _angles(pos, d, theta):
    inv = theta ** (-jnp.arange(0, d, 2, dtype=F32) / d)
    ang = pos.astype(F32)[:, None] * inv[None, :]
    return jnp.cos(ang), jnp.sin(ang)


def _mixer_tables(pos):
    n = pos.shape[0]
    one = lambda w: jnp.ones((n, w), F32)
    zero = lambda w: jnp.zeros((n, w), F32)
    cos, sin = _angles(pos, DIFF_ROT, ROPE_THETA)
    rest = DIFF_HD - DIFF_ROT
    dc = jnp.concatenate([cos, cos, one(rest)] * 2, axis=1)
    ds1 = jnp.concatenate([-sin, zero(DIFF_HD - 8)] * 2, axis=1)
    ds2 = jnp.concatenate([zero(8), sin, zero(rest)] * 2, axis=1)
    cos, sin = _angles(pos, MLA_ROPE, MLA_THETA)
    mc = jnp.concatenate([one(MLA_NOPE), cos, cos, one(LANES - MLA_QK)], axis=1)
    ms1 = jnp.concatenate([zero(MLA_NOPE), -sin, zero(LANES - MLA_NOPE - 16)], axis=1)
    ms2 = jnp.concatenate([zero(MLA_NOPE + 16), sin, zero(LANES - MLA_QK)], axis=1)
    return {"dc": dc, "ds1": ds1, "ds2": ds2, "mc": mc, "ms1": ms1, "ms2": ms2}


def _paged_tables(pos):
    n = pos.shape[0]
    cos, sin = _angles(pos, MLA_ROPE, MLA_THETA)
    zero = jnp.zeros((n, LANES - MLA_ROPE), F32)
    return jnp.concatenate([cos, cos, zero], axis=1), jnp.concatenate([-sin, sin, zero], axis=1)


def _pick(n, pref):
    while n % pref:
        pref //= 2
    return pref


def kernel(x_prompt, x_sample, cache_diff_k, cache_diff_v, cache_mla_ckv, cache_mla_krope, cache_mem_k, cache_mem_v, page_table, mem_prompt, norm_attn, w_in, diff_q_norm, diff_k_norm, diff_lambda, diff_subln, mla_q_lora_norm, w_uq, mla_kv_lora_norm, w_ukv, mla_q_norm, mla_k_norm, mla_out_norm, w_o, norm_cross, norm_mem, w_cq, w_ck, w_cv, cross_q_norm, cross_k_norm, w_co, norm_mlp, w_up, w_down):
    params = dict(norm_attn=norm_attn, w_in=w_in, diff_q_norm=diff_q_norm, diff_k_norm=diff_k_norm,
                  diff_lambda=diff_lambda, diff_subln=diff_subln, mla_q_lora_norm=mla_q_lora_norm, w_uq=w_uq,
                  mla_kv_lora_norm=mla_kv_lora_norm, w_ukv=w_ukv, mla_q_norm=mla_q_norm, mla_k_norm=mla_k_norm,
                  mla_out_norm=mla_out_norm, w_o=w_o, norm_cross=norm_cross, norm_mem=norm_mem, w_cq=w_cq,
                  w_ck=w_ck, w_cv=w_cv, cross_q_norm=cross_q_norm, cross_k_norm=cross_k_norm, w_co=w_co,
                  norm_mlp=norm_mlp, w_up=w_up, w_down=w_down)
    depth = w_in.shape[0]
    bp, sp, _ = x_prompt.shape
    db, ts, _ = x_sample.shape
    n_pool = cache_diff_k.shape[1]
    n_pages = page_table.shape[1]
    mem_len = mem_prompt.shape[1]
    tp, tsm = bp * sp, db * ts

    tm_p = _pick(sp, 256)
    tm_s = _pick(tsm, 256)
    ptab = _mixer_tables(jnp.arange(sp))
    spos = n_pages * PAGE + jnp.arange(ts)
    stab = {k: jnp.tile(v, (tm_s // ts, 1)) for k, v in _mixer_tables(spos).items()}
    pages = _pick(n_pages, 8)
    pc, ps = _paged_tables(jnp.arange(n_pages * PAGE))
    pcn, psn = _paged_tables(spos)
    padn = lambda a: jnp.pad(a, ((0, PAGE - ts), (0, 0)))
    paged_tabs = {"c": pc, "s": ps, "cn": padn(pcn), "sn": padn(psn)}

    ck = cache_diff_k.reshape(depth, n_pool, 4 * PAGE, DIFF_HD)
    cv = cache_diff_v.reshape(depth, n_pool, 2 * PAGE, DIFF_VD)

    xp = x_prompt.reshape(tp, D_MODEL)
    xs = x_sample.reshape(tsm, D_MODEL)
    mem = mem_prompt.reshape(bp * mem_len, D_MODEL)
    tq_d, tq_m, tk = _pick(sp, 256), _pick(sp, 512), _pick(sp, 512)
    tm_c = _pick(sp, 512)
    nseq_s = _pick(db, 16)
    tm_mlp_p, tm_mlp_s = _pick(tp, 1024), _pick(tsm, 1024)

    p_rows, s_rows = [], []
    for l in range(depth):
        lw = _layer_weights(params, l)

        qd, dk, dv, kdb, vdb, qm, ckv, kr, km, vm = _mixer_call(xp, lw, ptab, sp // tm_p, tm_p)
        od = _flash_call(qd.reshape(bp, sp, 1024), kdb.reshape(bp, sp, 256), vdb.reshape(bp, sp, 256),
                         groups=4, tq=tq_d, tk=tk, name="diff_flash")
        om = _flash_call(qm.reshape(bp, sp, 512), km.reshape(bp, sp, 512), vm.reshape(bp, sp, 512),
                         groups=1, tq=tq_m, tk=tk, name="mla_flash")
        xp = _merge_call(od.reshape(tp, 1024), om.reshape(tp, 512), xp, lw, l, tm_p)
        mk, mv = _memkv_call(mem, lw)
        steps_per_batch = sp // tm_c
        xp = _cross_call(xp, mk.reshape(bp, mem_len, CROSS_WIDTH), mv.reshape(bp, mem_len, CROSS_WIDTH), lw,
                         nseq=1, tpm=tm_c, mem_map=lambda i: i // steps_per_batch)
        xp = _mlp_call(xp, lw, tm_mlp_p, 1024)
        p_rows.append((dk, dv, ckv, kr, mk, mv))

        qd, dk, dv, _, _, qm, ckv, kr, _, _ = _mixer_call(xs, lw, stab, 1, tm_s)
        qh = jnp.transpose(qd.reshape(db, ts, 2, 4, LANES), (0, 2, 3, 1, 4)).reshape(db, 2, 4 * ts, LANES)
        zq = jnp.zeros_like(qh[:, 0])
        qbd = jnp.concatenate([jnp.concatenate([qh[:, 0], zq], axis=-1),
                               jnp.concatenate([zq, qh[:, 1]], axis=-1)], axis=1)
        od = _diff_paged_call(l, page_table, qbd, dk.reshape(db, ts, 256), dv.reshape(db, ts, 256), ck, cv,
                              pages=pages)
        od = jnp.transpose(od.reshape(db, 8, ts, LANES), (0, 2, 1, 3)).reshape(tsm, 1024)
        lq = _mla_absorb_call(qm, lw["wabs"])
        lq = jnp.transpose(lq.reshape(db, ts, MLA_HEADS, 2 * LANES), (0, 2, 1, 3)).reshape(db, MLA_HEADS * ts, 2 * LANES)
        om = _mla_paged_call(l, page_table, lq, ckv.reshape(db, ts, KV_LORA), kr.reshape(db, ts, MLA_ROPE),
                             cache_mla_ckv, cache_mla_krope, lw, paged_tabs, pages=pages)
        om = jnp.transpose(om.reshape(db, MLA_HEADS, ts, LANES), (0, 2, 1, 3)).reshape(tsm, 512)
        xs = _merge_call(od, om, xs, lw, l, tm_s)
        xs = _cross_call(xs, cache_mem_k[l].reshape(db, mem_len, CROSS_WIDTH),
                         cache_mem_v[l].reshape(db, mem_len, CROSS_WIDTH), lw,
                         nseq=nseq_s, tpm=ts, mem_map=lambda i: i)
        xs = _mlp_call(xs, lw, tm_mlp_s, 1024)
        s_rows.append((dk, dv, ckv, kr))

    st = lambda rows, i, shp: jnp.stack([r[i] for r in rows]).reshape((depth,) + shp)
    return (xp.reshape(bp, sp, D_MODEL), xs.reshape(db, ts, D_MODEL),
            st(p_rows, 0, (bp, sp, DIFF_KV_HEADS, 2, DIFF_HD)), st(p_rows, 1, (bp, sp, DIFF_KV_HEADS, DIFF_VD)),
            st(p_rows, 2, (bp, sp, KV_LORA)), st(p_rows, 3, (bp, sp, MLA_ROPE)),
            st(p_rows, 4, (bp, mem_len, CROSS_HEADS, CROSS_HD)), st(p_rows, 5, (bp, mem_len, CROSS_HEADS, CROSS_HD)),
            st(s_rows, 0, (db, ts, DIFF_KV_HEADS, 2, DIFF_HD)), st(s_rows, 1, (db, ts, DIFF_KV_HEADS, DIFF_VD)),
            st(s_rows, 2, (db, ts, KV_LORA)), st(s_rows, 3, (db, ts, MLA_ROPE)))
```

```python
import functools
import math

import jax
import jax.numpy as jnp
from jax import lax
from jax.experimental import pallas as pl
from jax.experimental.pallas import tpu as pltpu

F32 = jnp.float32
BF16 = jnp.bfloat16

LANES = 128
D_MODEL = 1024
PAGE = 128
DIFF_KV_HEADS = 2
DIFF_GROUP = 2
DIFF_HD = 64
DIFF_VD = 128
DIFF_ROT = 16
ROPE_THETA = 500000.0
MLA_HEADS = 4
Q_LORA = 256
KV_LORA = 128
MLA_NOPE = 64
MLA_ROPE = 32
MLA_QK = MLA_NOPE + MLA_ROPE
MLA_VD = 128
MLA_THETA = 10000.0
N_DQ = 512
N_DK = 256
N_DV = 256
IN_COLS = N_DQ + N_DK + N_DV + Q_LORA + KV_LORA + MLA_ROPE
IN_COLS_PAD = 1536
CROSS_HEADS = 4
CROSS_HD = 64
CROSS_WIDTH = 256
D_FF = 4096
EPS = 1e-6
NEG = -1e30
VMEM_LIMIT = 48 * 1024 * 1024
GROUP_PAGES = 4


def _cparams(sem):
    return pltpu.CompilerParams(dimension_semantics=sem, vmem_limit_bytes=VMEM_LIMIT)


def _rms(x, g):
    return x * lax.rsqrt(jnp.mean(x * x, axis=-1, keepdims=True) + EPS) * g


def _seg_norm(v, bmat, n, g):
    ssq = jnp.dot((v * v).astype(BF16), bmat, preferred_element_type=F32)
    return v * lax.rsqrt(ssq * (1.0 / n) + EPS) * g


def _rope_groups(v, c, s1, s2, sh):
    outs = []
    for j in range(v.shape[1] // LANES):
        g = v[:, j * LANES:(j + 1) * LANES]
        outs.append(g * c + pltpu.roll(g, LANES - sh, 1) * s1 + pltpu.roll(g, sh, 1) * s2)
    return jnp.concatenate(outs, axis=1)


def _dot(a, b):
    return jnp.dot(a, b, preferred_element_type=F32)


def _dot_nt(a, b):
    return lax.dot_general(a, b, (((1,), (1,)), ((), ())), preferred_element_type=F32)


def _mixer_kernel(x_ref, ga_ref, win_ref, gq_ref, gk_ref, b64_ref, dc_ref, ds1_ref, ds2_ref,
                  gql_ref, wuq_ref, gmq_ref, b128_ref, mc_ref, ms1_ref, ms2_ref,
                  gkv_ref, wuk_ref, wuv_ref, gmk_ref,
                  qd_ref, dk_ref, dv_ref, kdb_ref, vdb_ref, qm_ref, ckv_ref, kr_ref, km_ref, vm_ref):
    tm = x_ref.shape[0]
    h = _rms(x_ref[...], ga_ref[...]).astype(BF16)
    z = _dot(h, win_ref[...])
    dq, dk, dv = z[:, 0:512], z[:, 512:768], z[:, 768:1024]
    cq, ckv, krg = z[:, 1024:1280], z[:, 1280:1408], z[:, 1408:1536]
    dc, ds1, ds2 = dc_ref[...], ds1_ref[...], ds2_ref[...]
    b64 = b64_ref[...]

    dqn = _rope_groups(_seg_norm(dq, b64, DIFF_HD, gq_ref[...]), dc, ds1, ds2, DIFF_ROT // 2)
    dqn = dqn * (DIFF_HD ** -0.5)
    lane = lax.broadcasted_iota(jnp.int32, (tm, LANES), 1)
    for j in range(4):
        grp = dqn[:, j * LANES:(j + 1) * LANES]
        qd_ref[:, (2 * j) * LANES:(2 * j + 1) * LANES] = jnp.where(lane < DIFF_HD, grp, 0.0).astype(BF16)
        qd_ref[:, (2 * j + 1) * LANES:(2 * j + 2) * LANES] = jnp.where(lane >= DIFF_HD, grp, 0.0).astype(BF16)

    dkn = _rope_groups(_seg_norm(dk, b64[0:256, 0:256], DIFF_HD, gk_ref[...]), dc, ds1, ds2, DIFF_ROT // 2)
    dk_ref[...] = dkn
    kdb_ref[...] = dkn.astype(BF16)
    dv_ref[...] = dv
    vdb_ref[...] = dv.astype(BF16)

    mc, ms1, ms2 = mc_ref[...], ms1_ref[...], ms2_ref[...]
    b128 = b128_ref[...]
    q = _dot(_rms(cq, gql_ref[...]).astype(BF16), wuq_ref[...])
    qn = _rope_groups(_seg_norm(q, b128, MLA_QK, gmq_ref[...]), mc, ms1, ms2, MLA_ROPE // 2)
    qm_ref[...] = (qn * (MLA_QK ** -0.5)).astype(BF16)

    ckvn = _rms(ckv, gkv_ref[...])
    ckv_ref[...] = ckvn
    kr_ref[...] = krg[:, 0:MLA_ROPE]
    cb = ckvn.astype(BF16)
    krs = pltpu.roll(krg, MLA_NOPE, 1)
    kfull = _dot(cb, wuk_ref[...]) + jnp.concatenate([krs] * MLA_HEADS, axis=1)
    kn = _rope_groups(_seg_norm(kfull, b128, MLA_QK, gmk_ref[...]), mc, ms1, ms2, MLA_ROPE // 2)
    km_ref[...] = kn.astype(BF16)
    vm_ref[...] = _dot(cb, wuv_ref[...]).astype(BF16)


def _mixer_call(x, lw, tabs, n_tab_blocks, tm):
    t = x.shape[0]
    nt = t // tm
    tok = lambda w: pl.BlockSpec((tm, w), lambda i: (i, 0))
    full = lambda a: pl.BlockSpec(a.shape, lambda i: (0,) * a.ndim)
    tab = pl.BlockSpec((tm, LANES), lambda i: (i % n_tab_blocks, 0))
    ins = [x, lw["g_attn"], lw["w_in"], lw["g_dq"], lw["g_dk"], lw["b64"], tabs["dc"], tabs["ds1"], tabs["ds2"],
           lw["g_ql"], lw["w_uq"], lw["g_mq"], lw["b128"], tabs["mc"], tabs["ms1"], tabs["ms2"],
           lw["g_kvl"], lw["w_uk"], lw["w_uv"], lw["g_mk"]]
    in_specs = [tok(D_MODEL)] + [full(a) for a in ins[1:6]] + [tab] * 3 + [full(a) for a in ins[9:13]] \
        + [tab] * 3 + [full(a) for a in ins[16:20]]
    outs = [(1024, BF16), (256, F32), (256, F32), (256, BF16), (256, BF16), (512, BF16), (128, F32),
            (MLA_ROPE, F32), (512, BF16), (512, BF16)]
    return pl.pallas_call(
        _mixer_kernel,
        grid=(nt,),
        in_specs=in_specs,
        out_specs=[tok(w) for w, _ in outs],
        out_shape=[jax.ShapeDtypeStruct((t, w), d) for w, d in outs],
        compiler_params=_cparams(("parallel",)),
        name="mixer_inputs",
    )(*ins)


def _flash_kernel(qa_ref, ka_ref, fl_ref, q_ref, k_ref, v_ref, o_ref, qs, m_s, acc, *, groups, tq, tk):
    step = pl.program_id(2)
    qi, ki, fl = qa_ref[step], ka_ref[step], fl_ref[step]

    @pl.when(ki == 0)
    def _():
        for g in range(groups):
            qs[g * tq:(g + 1) * tq, :] = q_ref[:, g * LANES:(g + 1) * LANES]
        m_s[...] = jnp.full(m_s.shape, NEG, F32)
        acc[...] = jnp.zeros(acc.shape, F32)

    def update(masked):
        s = _dot_nt(qs[...], k_ref[...])
        if masked:
            row = lax.broadcasted_iota(jnp.int32, (tq, tk), 0) + qi * tq
            col = lax.broadcasted_iota(jnp.int32, (tq, tk), 1) + ki * tk
            keep = (col <= row)[None]
            s = jnp.where(keep, s.reshape(groups, tq, tk), NEG).reshape(groups * tq, tk)
        m_prev = m_s[...]
        m_new = jnp.maximum(m_prev, jnp.max(s, axis=1, keepdims=True))
        alpha = jnp.exp(m_prev - m_new)
        p = jnp.exp(s - jnp.concatenate([m_new] * (tk // LANES), axis=1)).astype(BF16)
        vext = jnp.concatenate([v_ref[...], jnp.ones((tk, LANES), BF16)], axis=1)
        acc[...] = jnp.concatenate([alpha, alpha], axis=1) * acc[...] + _dot(p, vext)
        m_s[...] = m_new

    pl.when((fl & 2) != 0)(lambda: update(True))
    pl.when((fl & 2) == 0)(lambda: update(False))

    @pl.when((fl & 1) != 0)
    def _():
        o = acc[:, 0:LANES] / acc[:, LANES:2 * LANES]
        for g in range(groups):
            o_ref[:, g * LANES:(g + 1) * LANES] = o[g * tq:(g + 1) * tq, :]


def _flash_call(q, k, v, *, groups, tq, tk, name):
    b, s, _ = q.shape
    hk = k.shape[2] // LANES
    qs_, ks_, fs_ = [], [], []
    for qi in range(s // tq):
        last = (qi * tq + tq - 1) // tk
        for ki in range(last + 1):
            diag = (ki * tk + tk - 1) > (qi * tq)
            qs_.append(qi)
            ks_.append(ki)
            fs_.append((1 if ki == last else 0) | (2 if diag else 0))
    nsteps = len(qs_)
    rows = groups * tq
    gs = pltpu.PrefetchScalarGridSpec(
        num_scalar_prefetch=3,
        grid=(b, hk, nsteps),
        in_specs=[pl.BlockSpec((None, tq, groups * LANES), lambda b_, h, t, qa, ka, fl: (b_, qa[t], h)),
                  pl.BlockSpec((None, tk, LANES), lambda b_, h, t, qa, ka, fl: (b_, ka[t], h)),
                  pl.BlockSpec((None, tk, LANES), lambda b_, h, t, qa, ka, fl: (b_, ka[t], h))],
        out_specs=pl.BlockSpec((None, tq, groups * LANES), lambda b_, h, t, qa, ka, fl: (b_, qa[t], h)),
        scratch_shapes=[pltpu.VMEM((rows, LANES), BF16), pltpu.VMEM((rows, LANES), F32),
                        pltpu.VMEM((rows, 2 * LANES), F32)],
    )
    return pl.pallas_call(
        functools.partial(_flash_kernel, groups=groups, tq=tq, tk=tk),
        grid_spec=gs,
        out_shape=jax.ShapeDtypeStruct(q.shape, F32),
        compiler_params=_cparams(("parallel", "parallel", "arbitrary")),
        name=name,
    )(jnp.array(qs_, jnp.int32), jnp.array(ks_, jnp.int32), jnp.array(fs_, jnp.int32), q, k, v)


def _merge_kernel(od_ref, om_ref, x_ref, lam_ref, gsub_ref, gmo_ref, wo_ref, y_ref, *, lam_init):
    lm = lam_ref[...]
    lam = (jnp.exp(jnp.sum(lm[0:1] * lm[1:2], axis=1, keepdims=True))
           - jnp.exp(jnp.sum(lm[2:3] * lm[3:4], axis=1, keepdims=True)) + lam_init)
    parts = []
    for j in range(4):
        o = od_ref[:, (2 * j) * LANES:(2 * j + 1) * LANES] - lam * od_ref[:, (2 * j + 1) * LANES:(2 * j + 2) * LANES]
        parts.append(_rms(o, gsub_ref[...]) * (1.0 - lam_init))
    for h in range(MLA_HEADS):
        parts.append(_rms(om_ref[:, h * LANES:(h + 1) * LANES], gmo_ref[...]))
    o = jnp.concatenate(parts, axis=1).astype(BF16)
    y_ref[...] = x_ref[...] + _dot(o, wo_ref[...])


def _merge_call(od, om, x, lw, layer, tm):
    t = x.shape[0]
    tok = lambda w: pl.BlockSpec((tm, w), lambda i: (i, 0))
    full = lambda a: pl.BlockSpec(a.shape, lambda i: (0,) * a.ndim)
    lam_init = 0.8 - 0.6 * math.exp(-0.3 * layer)
    ins = [od, om, x, lw["lam"], lw["g_sub"], lw["g_mo"], lw["w_o"]]
    return pl.pallas_call(
        functools.partial(_merge_kernel, lam_init=lam_init),
        grid=(t // tm,),
        in_specs=[tok(1024), tok(512), tok(D_MODEL)] + [full(a) for a in ins[3:]],
        out_specs=tok(D_MODEL),
        out_shape=jax.ShapeDtypeStruct((t, D_MODEL), F32),
        compiler_params=_cparams(("parallel",)),
        name="merge_out_proj",
    )(*ins)


def _memkv_kernel(mem_ref, gm_ref, wck_ref, wcv_ref, b64_ref, gk_ref, k_ref, v_ref):
    hm = _rms(mem_ref[...], gm_ref[...]).astype(BF16)
    k_ref[...] = _seg_norm(_dot(hm, wck_ref[...]), b64_ref[...], CROSS_HD, gk_ref[...])
    v_ref[...] = _dot(hm, wcv_ref[...])


def _memkv_call(mem, lw):
    t = mem.shape[0]
    tm = min(t, 256)
    tok = lambda w: pl.BlockSpec((tm, w), lambda i: (i, 0))
    full = lambda a: pl.BlockSpec(a.shape, lambda i: (0,) * a.ndim)
    ins = [mem, lw["g_mem"], lw["w_ck"], lw["w_cv"], lw["b64c"], lw["g_ck"]]
    return pl.pallas_call(
        _memkv_kernel,
        grid=(t // tm,),
        in_specs=[tok(D_MODEL)] + [full(a) for a in ins[1:]],
        out_specs=[tok(CROSS_WIDTH), tok(CROSS_WIDTH)],
        out_shape=[jax.ShapeDtypeStruct((t, CROSS_WIDTH), F32)] * 2,
        compiler_params=_cparams(("parallel",)),
        name="memory_kv",
    )(*ins)


def _cross_kernel(x_ref, gc_ref, wcq_ref, b64_ref, gq_ref, mk_ref, mv_ref, wco_ref, y_ref, *, nseq, tpm):
    x = x_ref[...]
    h = _rms(x, gc_ref[...]).astype(BF16)
    q = _seg_norm(_dot(h, wcq_ref[...]), b64_ref[...], CROSS_HD, gq_ref[...]) * (CROSS_HD ** -0.5)
    head_of_lane = lax.broadcasted_iota(jnp.int32, (tpm, CROSS_WIDTH), 1) // CROSS_HD
    outs = []
    for n in range(nseq):
        qn = q[n * tpm:(n + 1) * tpm, :]
        qbd = jnp.concatenate([jnp.where(head_of_lane == hh, qn, 0.0) for hh in range(CROSS_HEADS)], axis=0)
        s = _dot_nt(qbd.astype(BF16), mk_ref[n].astype(BF16))
        p = jnp.exp(s - jnp.max(s, axis=1, keepdims=True))
        p = p / jnp.sum(p, axis=1, keepdims=True)
        o = _dot(p.astype(BF16), mv_ref[n].astype(BF16))
        on = jnp.zeros((tpm, CROSS_WIDTH), F32)
        for hh in range(CROSS_HEADS):
            on = on + jnp.where(head_of_lane == hh, o[hh * tpm:(hh + 1) * tpm, :], 0.0)
        outs.append(on)
    o = jnp.concatenate(outs, axis=0) if nseq > 1 else outs[0]
    y_ref[...] = x + _dot(o.astype(BF16), wco_ref[...])


def _cross_call(x, mem_k, mem_v, lw, *, nseq, tpm, mem_map):
    t = x.shape[0]
    tm = nseq * tpm
    ml = mem_k.shape[1]
    tok = lambda w: pl.BlockSpec((tm, w), lambda i: (i, 0))
    full = lambda a: pl.BlockSpec(a.shape, lambda i: (0,) * a.ndim)
    mem = pl.BlockSpec((nseq, ml, CROSS_WIDTH), lambda i: (mem_map(i), 0, 0))
    ins = [x, lw["g_cross"], lw["w_cq"], lw["b64c"], lw["g_cq"], mem_k, mem_v, lw["w_co"]]
    return pl.pallas_call(
        functools.partial(_cross_kernel, nseq=nseq, tpm=tpm),
        grid=(t // tm,),
        in_specs=[tok(D_MODEL)] + [full(a) for a in ins[1:5]] + [mem, mem, full(ins[7])],
        out_specs=tok(D_MODEL),
        out_shape=jax.ShapeDtypeStruct((t, D_MODEL), F32),
        compiler_params=_cparams(("parallel",)),
        name="cross_attention",
    )(*ins)


def _mlp_kernel(x_ref, g_ref, wup_ref, wdn_ref, y_ref, h_s, acc):
    f = pl.program_id(1)

    @pl.when(f == 0)
    def _():
        h_s[...] = _rms(x_ref[...], g_ref[...]).astype(BF16)
        acc[...] = x_ref[...]

    u = jnp.maximum(_dot(h_s[...], wup_ref[...]), 0.0)
    acc[...] += _dot((u * u).astype(BF16), wdn_ref[...])

    @pl.when(f == pl.num_programs(1) - 1)
    def _():
        y_ref[...] = acc[...]


def _mlp_call(x, lw, tm, tf):
    t = x.shape[0]
    return pl.pallas_call(
        _mlp_kernel,
        grid=(t // tm, D_FF // tf),
        in_specs=[pl.BlockSpec((tm, D_MODEL), lambda i, f: (i, 0)),
                  pl.BlockSpec((1, D_MODEL), lambda i, f: (0, 0)),
                  pl.BlockSpec((D_MODEL, tf), lambda i, f: (0, f)),
                  pl.BlockSpec((tf, D_MODEL), lambda i, f: (f, 0))],
        out_specs=pl.BlockSpec((tm, D_MODEL), lambda i, f: (i, 0)),
        out_shape=jax.ShapeDtypeStruct((t, D_MODEL), F32),
        scratch_shapes=[pltpu.VMEM((tm, D_MODEL), BF16), pltpu.VMEM((tm, D_MODEL), F32)],
        compiler_params=_cparams(("parallel", "arbitrary")),
        name="sqrelu_mlp",
    )(x, lw["g_mlp"], lw["w_up"], lw["w_down"])


def _softmax_step(groups, m_ref, l_ref, acc_ref):
    stats = []
    for s, vals in groups:
        m_g = jnp.max(s, axis=1, keepdims=True)
        p = jnp.exp(s - m_g)
        stats.append((m_g, jnp.sum(p, axis=1, keepdims=True), _dot(p.astype(BF16), vals)))
    m_prev = m_ref[0][:, 0:1]
    m_new = m_prev
    for m_g, _, _ in stats:
        m_new = jnp.maximum(m_new, m_g)
    alpha = jnp.exp(m_prev - m_new)
    l_new = alpha * l_ref[0][:, 0:1]
    acc_new = alpha * acc_ref[0]
    for m_g, l_g, o_g in stats:
        w = jnp.exp(m_g - m_new)
        l_new = l_new + w * l_g
        acc_new = acc_new + w * o_g
    acc_ref[0] = acc_new
    m_ref[0] = jnp.broadcast_to(m_new, m_ref.shape[1:])
    l_ref[0] = jnp.broadcast_to(l_new, l_ref.shape[1:])


def _diff_paged_kernel(pt_ref, q_ref, kn_ref, vn_ref, *refs, pages, t_new):
    k_refs, v_refs = refs[:pages], refs[pages:2 * pages]
    o_ref, m_s, l_s, acc = refs[2 * pages:]
    c = pl.program_id(1)
    q = q_ref[...]
    rows = q.shape[0]

    @pl.when(c == 0)
    def _():
        pad = jnp.zeros((PAGE - t_new, 2 * LANES), F32)
        kn = jnp.concatenate([kn_ref[...], pad], axis=0).astype(BF16)
        vn = jnp.concatenate([vn_ref[...], pad], axis=0).astype(BF16)
        s = _dot_nt(q, kn)
        tq_ = lax.broadcasted_iota(jnp.int32, s.shape, 0) % t_new
        col = lax.broadcasted_iota(jnp.int32, s.shape, 1)
        s = jnp.where(col <= tq_, s, NEG)
        m = jnp.max(s, axis=1, keepdims=True)
        p = jnp.exp(s - m)
        m_s[0] = jnp.broadcast_to(m, (rows, LANES))
        l_s[0] = jnp.broadcast_to(jnp.sum(p, axis=1, keepdims=True), (rows, LANES))
        acc[0] = _dot(p.astype(BF16), vn)

    def v_page(r):
        return jnp.concatenate([r[pl.ds(j, PAGE, stride=2), :] for j in range(2)], axis=1).astype(BF16)

    groups = []
    for i in range(0, pages, GROUP_PAGES):
        kt = jnp.concatenate([k_refs[i + j][...] for j in range(GROUP_PAGES)], axis=1).astype(BF16)
        vv = jnp.concatenate([v_page(v_refs[i + j]) for j in range(GROUP_PAGES)], axis=0)
        groups.append((_dot(q, kt), vv))
    _softmax_step(groups, m_s, l_s, acc)

    @pl.when(c == pl.num_programs(1) - 1)
    def _():
        o = acc[0] / l_s[0][:, 0:1]
        half = rows // 2
        o_ref[...] = jnp.concatenate([o[0:half, 0:LANES], o[half:rows, LANES:2 * LANES]], axis=0)


def _diff_paged_call(layer, pt, qbd, k_new, v_new, cache_k, cache_v, *, pages):
    b, n_pages = pt.shape
    rows = qbd.shape[1]
    t_new = k_new.shape[1]

    def page_map(b_, c, pt_, *, i):
        return (layer, pt_[b_, c * pages + i], 0, 0)

    seq = lambda shp: pl.BlockSpec((None,) + shp, lambda b_, c, pt_: (b_, 0, 0))
    gs = pltpu.PrefetchScalarGridSpec(
        num_scalar_prefetch=1,
        grid=(b, n_pages // pages),
        in_specs=[seq((rows, 2 * LANES)), seq((t_new, 2 * LANES)), seq((t_new, 2 * LANES))]
        + [pl.BlockSpec((None, None, 2 * LANES, PAGE), functools.partial(page_map, i=i)) for i in range(pages)]
        + [pl.BlockSpec((None, None, 2 * PAGE, DIFF_VD), functools.partial(page_map, i=i)) for i in range(pages)],
        out_specs=seq((rows, LANES)),
        scratch_shapes=[pltpu.VMEM((1, rows, LANES), F32), pltpu.VMEM((1, rows, LANES), F32),
                        pltpu.VMEM((1, rows, 2 * LANES), F32)],
    )
    return pl.pallas_call(
        functools.partial(_diff_paged_kernel, pages=pages, t_new=t_new),
        grid_spec=gs,
        out_shape=jax.ShapeDtypeStruct((b, rows, LANES), F32),
        compiler_params=_cparams(("parallel", "arbitrary")),
        name="diff_paged_attention",
    )(pt, qbd, k_new, v_new, *([cache_k] * pages), *([cache_v] * pages))


def _mla_absorb_kernel(qm_ref, wabs_ref, lq_ref):
    for h in range(MLA_HEADS):
        lq_ref[:, h * 2 * LANES:(h + 1) * 2 * LANES] = _dot(
            qm_ref[:, h * LANES:(h + 1) * LANES], wabs_ref[h]).astype(BF16)


def _mla_absorb_call(qm, wabs):
    t = qm.shape[0]
    tm = min(t, 256)
    return pl.pallas_call(
        _mla_absorb_kernel,
        grid=(t // tm,),
        in_specs=[pl.BlockSpec((tm, MLA_HEADS * LANES), lambda i: (i, 0)),
                  pl.BlockSpec(wabs.shape, lambda i: (0, 0, 0))],
        out_specs=pl.BlockSpec((tm, MLA_HEADS * 2 * LANES), lambda i: (i, 0)),
        out_shape=jax.ShapeDtypeStruct((t, MLA_HEADS * 2 * LANES), BF16),
        compiler_params=_cparams(("parallel",)),
        name="mla_absorb_q",
    )(qm, wabs)


def _mla_paged_kernel(pt_ref, lq_ref, lr_ref, wrow_ref, cn_ref, rn_ref, tcn_ref, tsn_ref, tc_ref, ts_ref,
                      gr_ref, wuv_ref, *refs, pages, t_new):
    c_refs, r_refs = refs[:pages], refs[pages:2 * pages]
    o_ref, m_s, l_s, acc = refs[2 * pages:]
    c = pl.program_id(1)
    l_lat = jnp.concatenate([lq_ref[...], wrow_ref[...]], axis=0)
    l_rope = lr_ref[...]
    nq = lq_ref.shape[0]
    gr = gr_ref[...]
    half = MLA_ROPE // 2

    def scores(ckv_b, kr_t, tcos, tsin):
        x = kr_t * gr
        x1, x2 = x[0:half], x[half:MLA_ROPE]
        roped = jnp.concatenate([x1 * tcos - x2 * tsin, x2 * tcos + x1 * tsin], axis=0).astype(BF16)
        ss_rope = jnp.sum(kr_t * kr_t, axis=0, keepdims=True)
        out = _dot_nt(l_lat, ckv_b)
        raw = out[0:nq] + _dot(l_rope, roped)
        parts = []
        for h in range(MLA_HEADS):
            kn = out[nq + h * MLA_NOPE:nq + (h + 1) * MLA_NOPE, :]
            ss = jnp.sum(kn * kn, axis=0, keepdims=True) + ss_rope
            parts.append(raw[h * t_new:(h + 1) * t_new, :] * lax.rsqrt(ss * (1.0 / MLA_QK) + EPS))
        return jnp.concatenate(parts, axis=0)

    @pl.when(c == 0)
    def _():
        padc = jnp.zeros((PAGE - t_new, KV_LORA), F32)
        cb = jnp.concatenate([cn_ref[...], padc], axis=0).astype(BF16)
        s = scores(cb, rn_ref[...], tcn_ref[...], tsn_ref[...])
        tq_ = lax.broadcasted_iota(jnp.int32, s.shape, 0) % t_new
        col = lax.broadcasted_iota(jnp.int32, s.shape, 1)
        s = jnp.where(col <= tq_, s, NEG)
        m = jnp.max(s, axis=1, keepdims=True)
        p = jnp.exp(s - m)
        m_s[0] = jnp.broadcast_to(m, (nq, LANES))
        l_s[0] = jnp.broadcast_to(jnp.sum(p, axis=1, keepdims=True), (nq, LANES))
        acc[0] = _dot(p.astype(BF16), cb)

    groups = []
    gw = GROUP_PAGES * PAGE
    for i in range(0, pages, GROUP_PAGES):
        cb = jnp.concatenate([c_refs[i + j][...] for j in range(GROUP_PAGES)], axis=0).astype(BF16)
        kr_t = jnp.concatenate([r_refs[i + j][...] for j in range(GROUP_PAGES)], axis=1)
        off = pl.multiple_of((c * pages + i) * PAGE, gw)
        groups.append((scores(cb, kr_t, tc_ref[:, pl.ds(off, gw)], ts_ref[:, pl.ds(off, gw)]), cb))
    _softmax_step(groups, m_s, l_s, acc)

    @pl.when(c == pl.num_programs(1) - 1)
    def _():
        o = (acc[0] / l_s[0][:, 0:1]).astype(BF16)
        o_ref[...] = jnp.concatenate(
            [_dot(o[h * t_new:(h + 1) * t_new, :], wuv_ref[h]) for h in range(MLA_HEADS)], axis=0)


def _mla_paged_call(layer, pt, lq, ckv_new, kr_new, cache_ckv, cache_kr, lw, ptabs, *, pages):
    b, n_pages = pt.shape
    nq = lq.shape[1]
    t_new = ckv_new.shape[1]
    lq_lat = lq[:, :, 0:KV_LORA]
    lq_rope = lq[:, :, KV_LORA:KV_LORA + MLA_ROPE]

    def page_map(b_, c, pt_, *, i):
        return (layer, pt_[b_, c * pages + i], 0, 0)

    seq = lambda shp: pl.BlockSpec((None,) + shp, lambda b_, c, pt_: (b_, 0, 0))
    full = lambda a: pl.BlockSpec(a.shape, lambda b_, c, pt_: (0,) * a.ndim)
    consts = [ptabs["cn"], ptabs["sn"], ptabs["c"], ptabs["s"], lw["g_kr"], lw["w_uvh"]]
    gs = pltpu.PrefetchScalarGridSpec(
        num_scalar_prefetch=1,
        grid=(b, n_pages // pages),
        in_specs=[seq((nq, KV_LORA)), seq((nq, MLA_ROPE)), full(lw["w_rows"]), seq((t_new, KV_LORA)),
                  seq((MLA_ROPE, PAGE))] + [full(a) for a in consts]
        + [pl.BlockSpec((None, None, PAGE, KV_LORA), functools.partial(page_map, i=i)) for i in range(pages)]
        + [pl.BlockSpec((None, None, MLA_ROPE, PAGE), functools.partial(page_map, i=i)) for i in range(pages)],
        out_specs=seq((nq, LANES)),
        scratch_shapes=[pltpu.VMEM((1, nq, LANES), F32), pltpu.VMEM((1, nq, LANES), F32),
                        pltpu.VMEM((1, nq, LANES), F32)],
    )
    return pl.pallas_call(
        functools.partial(_mla_paged_kernel, pages=pages, t_new=t_new),
        grid_spec=gs,
        out_shape=jax.ShapeDtypeStruct((b, nq, LANES), F32),
        compiler_params=_cparams(("parallel", "arbitrary")),
        name="mla_paged_attention",
    )(pt, lq_lat, lq_rope, lw["w_rows"], ckv_new, kr_new, *consts,
      *([cache_ckv] * pages), *([cache_kr] * pages))


def _block_ones(n, seg):
    i = jnp.arange(n) // seg
    return (i[:, None] == i[None, :]).astype(BF16)


def _pad_heads(w, used):
    lead = w.shape[:-1]
    w = w.reshape(lead + (MLA_HEADS, used))
    w = jnp.pad(w, [(0, 0)] * len(lead) + [(0, 0), (0, LANES - used)])
    return w.reshape(lead + (MLA_HEADS * LANES,))


def _layer_weights(p, l):
    row = lambda a: a[l][None, :].astype(F32)
    w_ukv = p["w_ukv"][l].reshape(KV_LORA, MLA_HEADS, MLA_NOPE + MLA_VD)
    w_uk = w_ukv[:, :, :MLA_NOPE]
    w_uv = w_ukv[:, :, MLA_NOPE:]
    g_mk = p["mla_k_norm"][l]
    pad_g = lambda g: jnp.tile(jnp.pad(g, (0, LANES - MLA_QK)), MLA_HEADS)[None, :]
    wabs = jnp.zeros((MLA_HEADS, LANES, 2 * LANES), F32)
    wabs = wabs.at[:, :MLA_NOPE, :KV_LORA].set(jnp.transpose(w_uk, (1, 2, 0)) * g_mk[None, :MLA_NOPE, None])
    wabs = wabs.at[:, MLA_NOPE + jnp.arange(MLA_ROPE), KV_LORA + jnp.arange(MLA_ROPE)].set(1.0)
    w_rows = jnp.transpose(w_uk, (1, 2, 0)).reshape(MLA_HEADS * MLA_NOPE, KV_LORA)
    return {
        "g_attn": row(p["norm_attn"]),
        "w_in": jnp.pad(p["w_in"][l], ((0, 0), (0, IN_COLS_PAD - IN_COLS))).astype(BF16),
        "g_dq": jnp.tile(p["diff_q_norm"][l], 8)[None, :],
        "g_dk": jnp.tile(p["diff_k_norm"][l], 4)[None, :],
        "b64": _block_ones(512, 64),
        "b128": _block_ones(512, 128),
        "b64c": _block_ones(256, 64),
        "g_ql": row(p["mla_q_lora_norm"]),
        "w_uq": _pad_heads(p["w_uq"][l], MLA_QK).astype(BF16),
        "g_mq": pad_g(p["mla_q_norm"][l]),
        "g_kvl": row(p["mla_kv_lora_norm"]),
        "w_uk": _pad_heads(w_uk.reshape(KV_LORA, MLA_HEADS * MLA_NOPE), MLA_NOPE).astype(BF16),
        "w_uv": w_uv.reshape(KV_LORA, MLA_HEADS * MLA_VD).astype(BF16),
        "g_mk": pad_g(g_mk),
        "lam": p["diff_lambda"][l],
        "g_sub": row(p["diff_subln"]),
        "g_mo": row(p["mla_out_norm"]),
        "w_o": p["w_o"][l].astype(BF16),
        "g_mem": row(p["norm_mem"]),
        "w_ck": p["w_ck"][l].astype(BF16),
        "w_cv": p["w_cv"][l].astype(BF16),
        "g_ck": jnp.tile(p["cross_k_norm"][l], 4)[None, :],
        "g_cross": row(p["norm_cross"]),
        "w_cq": p["w_cq"][l].astype(BF16),
        "g_cq": jnp.tile(p["cross_q_norm"][l], 4)[None, :],
        "w_co": p["w_co"][l].astype(BF16),
        "g_mlp": row(p["norm_mlp"]),
        "w_up": p["w_up"][l].astype(BF16),
        "w_down": p["w_down"][l].astype(BF16),
        "wabs": wabs.astype(BF16),
        "w_rows": w_rows.astype(BF16),
        "g_kr": g_mk[MLA_NOPE:, None],
        "w_uvh": jnp.transpose(w_uv, (1, 0, 2)).astype(BF16),
    }


def _angles(pos, d, theta):
    inv = theta ** (-jnp.arange(0, d, 2, dtype=F32) / d)
    ang = pos.astype(F32)[:, None] * inv[None, :]
    return jnp.cos(ang), jnp.sin(ang)


def _mixer_tables(pos):
    n = pos.shape[0]
    one = lambda w: jnp.ones((n, w), F32)
    zero = lambda w: jnp.zeros((n, w), F32)
    cos, sin = _angles(pos, DIFF_ROT, ROPE_THETA)
    rest = DIFF_HD - DIFF_ROT
    dc = jnp.concatenate([cos, cos, one(rest)] * 2, axis=1)
    ds1 = jnp.concatenate([-sin, zero(DIFF_HD - 8)] * 2, axis=1)
    ds2 = jnp.concatenate([zero(8), sin, zero(rest)] * 2, axis=1)
    cos, sin = _angles(pos, MLA_ROPE, MLA_THETA)
    mc = jnp.concatenate([one(MLA_NOPE), cos, cos, one(LANES - MLA_QK)], axis=1)
    ms1 = jnp.concatenate([zero(MLA_NOPE), -sin, zero(LANES - MLA_NOPE - 16)], axis=1)
    ms2 = jnp.concatenate([zero(MLA_NOPE + 16), sin, zero(LANES - MLA_QK)], axis=1)
    return {"dc": dc, "ds1": ds1, "ds2": ds2, "mc": mc, "ms1": ms1, "ms2": ms2}


def _paged_tables(pos):
    cos, sin = _angles(pos, MLA_ROPE, MLA_THETA)
    return cos.T, sin.T


def _pick(n, pref):
    while n % pref:
        pref //= 2
    return pref


def kernel(x_prompt, x_sample, cache_diff_k, cache_diff_v, cache_mla_ckv, cache_mla_krope, cache_mem_k, cache_mem_v, page_table, mem_prompt, norm_attn, w_in, diff_q_norm, diff_k_norm, diff_lambda, diff_subln, mla_q_lora_norm, w_uq, mla_kv_lora_norm, w_ukv, mla_q_norm, mla_k_norm, mla_out_norm, w_o, norm_cross, norm_mem, w_cq, w_ck, w_cv, cross_q_norm, cross_k_norm, w_co, norm_mlp, w_up, w_down):
    params = dict(norm_attn=norm_attn, w_in=w_in, diff_q_norm=diff_q_norm, diff_k_norm=diff_k_norm,
                  diff_lambda=diff_lambda, diff_subln=diff_subln, mla_q_lora_norm=mla_q_lora_norm, w_uq=w_uq,
                  mla_kv_lora_norm=mla_kv_lora_norm, w_ukv=w_ukv, mla_q_norm=mla_q_norm, mla_k_norm=mla_k_norm,
                  mla_out_norm=mla_out_norm, w_o=w_o, norm_cross=norm_cross, norm_mem=norm_mem, w_cq=w_cq,
                  w_ck=w_ck, w_cv=w_cv, cross_q_norm=cross_q_norm, cross_k_norm=cross_k_norm, w_co=w_co,
                  norm_mlp=norm_mlp, w_up=w_up, w_down=w_down)
    depth = w_in.shape[0]
    bp, sp, _ = x_prompt.shape
    db, ts, _ = x_sample.shape
    n_pool = cache_diff_k.shape[1]
    n_pages = page_table.shape[1]
    mem_len = mem_prompt.shape[1]
    tp, tsm = bp * sp, db * ts

    tm_p = _pick(sp, 256)
    tm_s = _pick(tsm, 256)
    ptab = _mixer_tables(jnp.arange(sp))
    spos = n_pages * PAGE + jnp.arange(ts)
    stab = {k: jnp.tile(v, (tm_s // ts, 1)) for k, v in _mixer_tables(spos).items()}
    pages = _pick(n_pages, 16)
    pc, ps = _paged_tables(jnp.arange(n_pages * PAGE))
    pcn, psn = _paged_tables(spos)
    padn = lambda a: jnp.pad(a, ((0, 0), (0, PAGE - ts)))
    paged_tabs = {"c": pc, "s": ps, "cn": padn(pcn), "sn": padn(psn)}

    ck = jnp.transpose(cache_diff_k, (0, 1, 3, 4, 5, 2)).reshape(depth, n_pool, 2 * LANES, PAGE)
    cv = cache_diff_v.reshape(depth, n_pool, 2 * PAGE, DIFF_VD)
    ckr = jnp.transpose(cache_mla_krope, (0, 1, 3, 2))

    xp = x_prompt.reshape(tp, D_MODEL)
    xs = x_sample.reshape(tsm, D_MODEL)
    mem = mem_prompt.reshape(bp * mem_len, D_MODEL)
    tq_d, tq_m, tk = _pick(sp, 256), _pick(sp, 512), _pick(sp, 512)
    tm_c = _pick(sp, 512)
    nseq_s = _pick(db, 16)
    tm_mlp_p, tm_mlp_s = _pick(tp, 1024), _pick(tsm, 1024)

    p_rows, s_rows = [], []
    for l in range(depth):
        lw = _layer_weights(params, l)

        qd, dk, dv, kdb, vdb, qm, ckv, kr, km, vm = _mixer_call(xp, lw, ptab, sp // tm_p, tm_p)
        od = _flash_call(qd.reshape(bp, sp, 1024), kdb.reshape(bp, sp, 256), vdb.reshape(bp, sp, 256),
                         groups=4, tq=tq_d, tk=tk, name="diff_flash")
        om = _flash_call(qm.reshape(bp, sp, 512), km.reshape(bp, sp, 512), vm.reshape(bp, sp, 512),
                         groups=1, tq=tq_m, tk=tk, name="mla_flash")
        xp = _merge_call(od.reshape(tp, 1024), om.reshape(tp, 512), xp, lw, l, tm_p)
        mk, mv = _memkv_call(mem, lw)
        steps_per_batch = sp // tm_c
        xp = _cross_call(xp, mk.reshape(bp, mem_len, CROSS_WIDTH), mv.reshape(bp, mem_len, CROSS_WIDTH), lw,
                         nseq=1, tpm=tm_c, mem_map=lambda i: i // steps_per_batch)
        xp = _mlp_call(xp, lw, tm_mlp_p, 1024)
        p_rows.append((dk, dv, ckv, kr, mk, mv))

        qd, dk, dv, _, _, qm, ckv, kr, _, _ = _mixer_call(xs, lw, stab, 1, tm_s)
        qh = jnp.transpose(qd.reshape(db, ts, 2, 4, LANES), (0, 2, 3, 1, 4)).reshape(db, 2, 4 * ts, LANES)
        zq = jnp.zeros_like(qh[:, 0])
        qbd = jnp.concatenate([jnp.concatenate([qh[:, 0], zq], axis=-1),
                               jnp.concatenate([zq, qh[:, 1]], axis=-1)], axis=1)
        od = _diff_paged_call(l, page_table, qbd, dk.reshape(db, ts, 256), dv.reshape(db, ts, 256), ck, cv,
                              pages=pages)
        od = jnp.transpose(od.reshape(db, 8, ts, LANES), (0, 2, 1, 3)).reshape(tsm, 1024)
        lq = _mla_absorb_call(qm, lw["wabs"])
        lq = jnp.transpose(lq.reshape(db, ts, MLA_HEADS, 2 * LANES), (0, 2, 1, 3)).reshape(db, MLA_HEADS * ts, 2 * LANES)
        kr_t = jnp.pad(jnp.transpose(kr.reshape(db, ts, MLA_ROPE), (0, 2, 1)), ((0, 0), (0, 0), (0, PAGE - ts)))
        om = _mla_paged_call(l, page_table, lq, ckv.reshape(db, ts, KV_LORA), kr_t,
                             cache_mla_ckv, ckr, lw, paged_tabs, pages=pages)
        om = jnp.transpose(om.reshape(db, MLA_HEADS, ts, LANES), (0, 2, 1, 3)).reshape(tsm, 512)
        xs = _merge_call(od, om, xs, lw, l, tm_s)
        xs = _cross_call(xs, cache_mem_k[l].reshape(db, mem_len, CROSS_WIDTH),
                         cache_mem_v[l].reshape(db, mem_len, CROSS_WIDTH), lw,
                         nseq=nseq_s, tpm=ts, mem_map=lambda i: i)
        xs = _mlp_call(xs, lw, tm_mlp_s, 1024)
        s_rows.append((dk, dv, ckv, kr))

    st = lambda rows, i, shp: jnp.stack([r[i] for r in rows]).reshape((depth,) + shp)
    return (xp.reshape(bp, sp, D_MODEL), xs.reshape(db, ts, D_MODEL),
            st(p_rows, 0, (bp, sp, DIFF_KV_HEADS, 2, DIFF_HD)), st(p_rows, 1, (bp, sp, DIFF_KV_HEADS, DIFF_VD)),
            st(p_rows, 2, (bp, sp, KV_LORA)), st(p_rows, 3, (bp, sp, MLA_ROPE)),
            st(p_rows, 4, (bp, mem_len, CROSS_HEADS, CROSS_HD)), st(p_rows, 5, (bp, mem_len, CROSS_HEADS, CROSS_HD)),
            st(s_rows, 0, (db, ts, DIFF_KV_HEADS, 2, DIFF_HD)), st(s_rows, 1, (db, ts, DIFF_KV_HEADS, DIFF_VD)),
            st(s_rows, 2, (db, ts, KV_LORA)), st(s_rows, 3, (db, ts, MLA_ROPE)))
```

```python
import functools
import math

import jax
import jax.numpy as jnp
from jax import lax
from jax.experimental import pallas as pl
from jax.experimental.pallas import tpu as pltpu

F32 = jnp.float32
BF16 = jnp.bfloat16

LANES = 128
D_MODEL = 1024
PAGE = 128
DIFF_KV_HEADS = 2
DIFF_GROUP = 2
DIFF_HD = 64
DIFF_VD = 128
DIFF_ROT = 16
ROPE_THETA = 500000.0
MLA_HEADS = 4
Q_LORA = 256
KV_LORA = 128
MLA_NOPE = 64
MLA_ROPE = 32
MLA_QK = MLA_NOPE + MLA_ROPE
MLA_VD = 128
MLA_THETA = 10000.0
N_DQ = 512
N_DK = 256
N_DV = 256
IN_COLS = N_DQ + N_DK + N_DV + Q_LORA + KV_LORA + MLA_ROPE
IN_COLS_PAD = 1536
CROSS_HEADS = 4
CROSS_HD = 64
CROSS_WIDTH = 256
D_FF = 4096
EPS = 1e-6
NEG = -1e30
VMEM_LIMIT = 48 * 1024 * 1024
GROUP_PAGES = 4


def _cparams(sem):
    return pltpu.CompilerParams(dimension_semantics=sem, vmem_limit_bytes=VMEM_LIMIT)


def _rms(x, g):
    return x * lax.rsqrt(jnp.mean(x * x, axis=-1, keepdims=True) + EPS) * g


def _seg_norm(v, bmat, n, g):
    ssq = jnp.dot((v * v).astype(BF16), bmat, preferred_element_type=F32)
    return v * lax.rsqrt(ssq * (1.0 / n) + EPS) * g


def _rope_groups(v, c, s1, s2, sh):
    outs = []
    for j in range(v.shape[1] // LANES):
        g = v[:, j * LANES:(j + 1) * LANES]
        outs.append(g * c + pltpu.roll(g, LANES - sh, 1) * s1 + pltpu.roll(g, sh, 1) * s2)
    return jnp.concatenate(outs, axis=1)


def _dot(a, b):
    return jnp.dot(a, b, preferred_element_type=F32)


def _dot_nt(a, b):
    return lax.dot_general(a, b, (((1,), (1,)), ((), ())), preferred_element_type=F32)


def _mixer_kernel(x_ref, ga_ref, win_ref, gq_ref, gk_ref, b64_ref, dc_ref, ds1_ref, ds2_ref,
                  gql_ref, wuq_ref, gmq_ref, b128_ref, mc_ref, ms1_ref, ms2_ref,
                  gkv_ref, wuk_ref, wuv_ref, gmk_ref,
                  qd_ref, dk_ref, dv_ref, kdb_ref, vdb_ref, qm_ref, ckv_ref, kr_ref, km_ref, vm_ref):
    tm = x_ref.shape[0]
    h = _rms(x_ref[...], ga_ref[...]).astype(BF16)
    z = _dot(h, win_ref[...])
    dq, dk, dv = z[:, 0:512], z[:, 512:768], z[:, 768:1024]
    cq, ckv, krg = z[:, 1024:1280], z[:, 1280:1408], z[:, 1408:1536]
    dc, ds1, ds2 = dc_ref[...], ds1_ref[...], ds2_ref[...]
    b64 = b64_ref[...]

    dqn = _rope_groups(_seg_norm(dq, b64, DIFF_HD, gq_ref[...]), dc, ds1, ds2, DIFF_ROT // 2)
    dqn = dqn * (DIFF_HD ** -0.5)
    lane = lax.broadcasted_iota(jnp.int32, (tm, LANES), 1)
    for j in range(4):
        grp = dqn[:, j * LANES:(j + 1) * LANES]
        qd_ref[:, (2 * j) * LANES:(2 * j + 1) * LANES] = jnp.where(lane < DIFF_HD, grp, 0.0).astype(BF16)
        qd_ref[:, (2 * j + 1) * LANES:(2 * j + 2) * LANES] = jnp.where(lane >= DIFF_HD, grp, 0.0).astype(BF16)

    dkn = _rope_groups(_seg_norm(dk, b64[0:256, 0:256], DIFF_HD, gk_ref[...]), dc, ds1, ds2, DIFF_ROT // 2)
    dk_ref[...] = dkn
    kdb_ref[...] = dkn.astype(BF16)
    dv_ref[...] = dv
    vdb_ref[...] = dv.astype(BF16)

    mc, ms1, ms2 = mc_ref[...], ms1_ref[...], ms2_ref[...]
    b128 = b128_ref[...]
    q = _dot(_rms(cq, gql_ref[...]).astype(BF16), wuq_ref[...])
    qn = _rope_groups(_seg_norm(q, b128, MLA_QK, gmq_ref[...]), mc, ms1, ms2, MLA_ROPE // 2)
    qm_ref[...] = (qn * (MLA_QK ** -0.5)).astype(BF16)

    ckvn = _rms(ckv, gkv_ref[...])
    ckv_ref[...] = ckvn
    kr_ref[...] = krg[:, 0:MLA_ROPE]
    cb = ckvn.astype(BF16)
    krs = pltpu.roll(krg, MLA_NOPE, 1)
    kfull = _dot(cb, wuk_ref[...]) + jnp.concatenate([krs] * MLA_HEADS, axis=1)
    kn = _rope_groups(_seg_norm(kfull, b128, MLA_QK, gmk_ref[...]), mc, ms1, ms2, MLA_ROPE // 2)
    km_ref[...] = kn.astype(BF16)
    vm_ref[...] = _dot(cb, wuv_ref[...]).astype(BF16)


def _mixer_call(x, lw, tabs, n_tab_blocks, tm):
    t = x.shape[0]
    nt = t // tm
    tok = lambda w: pl.BlockSpec((tm, w), lambda i: (i, 0))
    full = lambda a: pl.BlockSpec(a.shape, lambda i: (0,) * a.ndim)
    tab = pl.BlockSpec((tm, LANES), lambda i: (i % n_tab_blocks, 0))
    ins = [x, lw["g_attn"], lw["w_in"], lw["g_dq"], lw["g_dk"], lw["b64"], tabs["dc"], tabs["ds1"], tabs["ds2"],
           lw["g_ql"], lw["w_uq"], lw["g_mq"], lw["b128"], tabs["mc"], tabs["ms1"], tabs["ms2"],
           lw["g_kvl"], lw["w_uk"], lw["w_uv"], lw["g_mk"]]
    in_specs = [tok(D_MODEL)] + [full(a) for a in ins[1:6]] + [tab] * 3 + [full(a) for a in ins[9:13]] \
        + [tab] * 3 + [full(a) for a in ins[16:20]]
    outs = [(1024, BF16), (256, F32), (256, F32), (256, BF16), (256, BF16), (512, BF16), (128, F32),
            (MLA_ROPE, F32), (512, BF16), (512, BF16)]
    return pl.pallas_call(
        _mixer_kernel,
        grid=(nt,),
        in_specs=in_specs,
        out_specs=[tok(w) for w, _ in outs],
        out_shape=[jax.ShapeDtypeStruct((t, w), d) for w, d in outs],
        compiler_params=_cparams(("parallel",)),
        name="mixer_inputs",
    )(*ins)


def _flash_kernel(qa_ref, ka_ref, fl_ref, q_ref, k_ref, v_ref, o_ref, qs, m_s, acc, *, groups, tq, tk):
    step = pl.program_id(2)
    qi, ki, fl = qa_ref[step], ka_ref[step], fl_ref[step]

    @pl.when(ki == 0)
    def _():
        for g in range(groups):
            qs[g * tq:(g + 1) * tq, :] = q_ref[:, g * LANES:(g + 1) * LANES]
        m_s[...] = jnp.full(m_s.shape, NEG, F32)
        acc[...] = jnp.zeros(acc.shape, F32)

    def update(masked):
        s = _dot_nt(qs[...], k_ref[...])
        if masked:
            row = lax.broadcasted_iota(jnp.int32, (tq, tk), 0) + qi * tq
            col = lax.broadcasted_iota(jnp.int32, (tq, tk), 1) + ki * tk
            keep = (col <= row)[None]
            s = jnp.where(keep, s.reshape(groups, tq, tk), NEG).reshape(groups * tq, tk)
        m_prev = m_s[...]
        m_new = jnp.maximum(m_prev, jnp.max(s, axis=1, keepdims=True))
        alpha = jnp.exp(m_prev - m_new)
        p = jnp.exp(s - jnp.concatenate([m_new] * (tk // LANES), axis=1)).astype(BF16)
        vext = jnp.concatenate([v_ref[...], jnp.ones((tk, LANES), BF16)], axis=1)
        acc[...] = jnp.concatenate([alpha, alpha], axis=1) * acc[...] + _dot(p, vext)
        m_s[...] = m_new

    pl.when((fl & 2) != 0)(lambda: update(True))
    pl.when((fl & 2) == 0)(lambda: update(False))

    @pl.when((fl & 1) != 0)
    def _():
        o = acc[:, 0:LANES] / acc[:, LANES:2 * LANES]
        for g in range(groups):
            o_ref[:, g * LANES:(g + 1) * LANES] = o[g * tq:(g + 1) * tq, :]


def _flash_call(q, k, v, *, groups, tq, tk, name):
    b, s, _ = q.shape
    hk = k.shape[2] // LANES
    qs_, ks_, fs_ = [], [], []
    for qi in range(s // tq):
        last = (qi * tq + tq - 1) // tk
        for ki in range(last + 1):
            diag = (ki * tk + tk - 1) > (qi * tq)
            qs_.append(qi)
            ks_.append(ki)
            fs_.append((1 if ki == last else 0) | (2 if diag else 0))
    nsteps = len(qs_)
    rows = groups * tq
    gs = pltpu.PrefetchScalarGridSpec(
        num_scalar_prefetch=3,
        grid=(b, hk, nsteps),
        in_specs=[pl.BlockSpec((None, tq, groups * LANES), lambda b_, h, t, qa, ka, fl: (b_, qa[t], h)),
                  pl.BlockSpec((None, tk, LANES), lambda b_, h, t, qa, ka, fl: (b_, ka[t], h)),
                  pl.BlockSpec((None, tk, LANES), lambda b_, h, t, qa, ka, fl: (b_, ka[t], h))],
        out_specs=pl.BlockSpec((None, tq, groups * LANES), lambda b_, h, t, qa, ka, fl: (b_, qa[t], h)),
        scratch_shapes=[pltpu.VMEM((rows, LANES), BF16), pltpu.VMEM((rows, LANES), F32),
                        pltpu.VMEM((rows, 2 * LANES), F32)],
    )
    return pl.pallas_call(
        functools.partial(_flash_kernel, groups=groups, tq=tq, tk=tk),
        grid_spec=gs,
        out_shape=jax.ShapeDtypeStruct(q.shape, F32),
        compiler_params=_cparams(("parallel", "parallel", "arbitrary")),
        name=name,
    )(jnp.array(qs_, jnp.int32), jnp.array(ks_, jnp.int32), jnp.array(fs_, jnp.int32), q, k, v)


def _merge_kernel(od_ref, om_ref, x_ref, lam_ref, gsub_ref, gmo_ref, wo_ref, y_ref, *, lam_init):
    lm = lam_ref[...]
    lam = (jnp.exp(jnp.sum(lm[0:1] * lm[1:2], axis=1, keepdims=True))
           - jnp.exp(jnp.sum(lm[2:3] * lm[3:4], axis=1, keepdims=True)) + lam_init)
    parts = []
    for j in range(4):
        o = od_ref[:, (2 * j) * LANES:(2 * j + 1) * LANES] - lam * od_ref[:, (2 * j + 1) * LANES:(2 * j + 2) * LANES]
        parts.append(_rms(o, gsub_ref[...]) * (1.0 - lam_init))
    for h in range(MLA_HEADS):
        parts.append(_rms(om_ref[:, h * LANES:(h + 1) * LANES], gmo_ref[...]))
    o = jnp.concatenate(parts, axis=1).astype(BF16)
    y_ref[...] = x_ref[...] + _dot(o, wo_ref[...])


def _merge_call(od, om, x, lw, layer, tm):
    t = x.shape[0]
    tok = lambda w: pl.BlockSpec((tm, w), lambda i: (i, 0))
    full = lambda a: pl.BlockSpec(a.shape, lambda i: (0,) * a.ndim)
    lam_init = 0.8 - 0.6 * math.exp(-0.3 * layer)
    ins = [od, om, x, lw["lam"], lw["g_sub"], lw["g_mo"], lw["w_o"]]
    return pl.pallas_call(
        functools.partial(_merge_kernel, lam_init=lam_init),
        grid=(t // tm,),
        in_specs=[tok(1024), tok(512), tok(D_MODEL)] + [full(a) for a in ins[3:]],
        out_specs=tok(D_MODEL),
        out_shape=jax.ShapeDtypeStruct((t, D_MODEL), F32),
        compiler_params=_cparams(("parallel",)),
        name="merge_out_proj",
    )(*ins)


def _memkv_kernel(mem_ref, gm_ref, wck_ref, wcv_ref, b64_ref, gk_ref, k_ref, v_ref):
    hm = _rms(mem_ref[...], gm_ref[...]).astype(BF16)
    k_ref[...] = _seg_norm(_dot(hm, wck_ref[...]), b64_ref[...], CROSS_HD, gk_ref[...])
    v_ref[...] = _dot(hm, wcv_ref[...])


def _memkv_call(mem, lw):
    t = mem.shape[0]
    tm = min(t, 256)
    tok = lambda w: pl.BlockSpec((tm, w), lambda i: (i, 0))
    full = lambda a: pl.BlockSpec(a.shape, lambda i: (0,) * a.ndim)
    ins = [mem, lw["g_mem"], lw["w_ck"], lw["w_cv"], lw["b64c"], lw["g_ck"]]
    return pl.pallas_call(
        _memkv_kernel,
        grid=(t // tm,),
        in_specs=[tok(D_MODEL)] + [full(a) for a in ins[1:]],
        out_specs=[tok(CROSS_WIDTH), tok(CROSS_WIDTH)],
        out_shape=[jax.ShapeDtypeStruct((t, CROSS_WIDTH), F32)] * 2,
        compiler_params=_cparams(("parallel",)),
        name="memory_kv",
    )(*ins)


def _cross_kernel(x_ref, gc_ref, wcq_ref, b64_ref, gq_ref, mk_ref, mv_ref, wco_ref, y_ref, *, nseq, tpm):
    x = x_ref[...]
    h = _rms(x, gc_ref[...]).astype(BF16)
    q = _seg_norm(_dot(h, wcq_ref[...]), b64_ref[...], CROSS_HD, gq_ref[...]) * (CROSS_HD ** -0.5)
    head_of_lane = lax.broadcasted_iota(jnp.int32, (tpm, CROSS_WIDTH), 1) // CROSS_HD
    outs = []
    for n in range(nseq):
        qn = q[n * tpm:(n + 1) * tpm, :]
        qbd = jnp.concatenate([jnp.where(head_of_lane == hh, qn, 0.0) for hh in range(CROSS_HEADS)], axis=0)
        s = _dot_nt(qbd.astype(BF16), mk_ref[n].astype(BF16))
        p = jnp.exp(s - jnp.max(s, axis=1, keepdims=True))
        p = p / jnp.sum(p, axis=1, keepdims=True)
        o = _dot(p.astype(BF16), mv_ref[n].astype(BF16))
        on = jnp.zeros((tpm, CROSS_WIDTH), F32)
        for hh in range(CROSS_HEADS):
            on = on + jnp.where(head_of_lane == hh, o[hh * tpm:(hh + 1) * tpm, :], 0.0)
        outs.append(on)
    o = jnp.concatenate(outs, axis=0) if nseq > 1 else outs[0]
    y_ref[...] = x + _dot(o.astype(BF16), wco_ref[...])


def _cross_call(x, mem_k, mem_v, lw, *, nseq, tpm, mem_map):
    t = x.shape[0]
    tm = nseq * tpm
    ml = mem_k.shape[1]
    tok = lambda w: pl.BlockSpec((tm, w), lambda i: (i, 0))
    full = lambda a: pl.BlockSpec(a.shape, lambda i: (0,) * a.ndim)
    mem = pl.BlockSpec((nseq, ml, CROSS_WIDTH), lambda i: (mem_map(i), 0, 0))
    ins = [x, lw["g_cross"], lw["w_cq"], lw["b64c"], lw["g_cq"], mem_k, mem_v, lw["w_co"]]
    return pl.pallas_call(
        functools.partial(_cross_kernel, nseq=nseq, tpm=tpm),
        grid=(t // tm,),
        in_specs=[tok(D_MODEL)] + [full(a) for a in ins[1:5]] + [mem, mem, full(ins[7])],
        out_specs=tok(D_MODEL),
        out_shape=jax.ShapeDtypeStruct((t, D_MODEL), F32),
        compiler_params=_cparams(("parallel",)),
        name="cross_attention",
    )(*ins)


def _mlp_kernel(x_ref, g_ref, wup_ref, wdn_ref, y_ref, h_s, acc):
    f = pl.program_id(1)

    @pl.when(f == 0)
    def _():
        h_s[...] = _rms(x_ref[...], g_ref[...]).astype(BF16)
        acc[...] = x_ref[...]

    u = jnp.maximum(_dot(h_s[...], wup_ref[...]), 0.0)
    acc[...] += _dot((u * u).astype(BF16), wdn_ref[...])

    @pl.when(f == pl.num_programs(1) - 1)
    def _():
        y_ref[...] = acc[...]


def _mlp_call(x, lw, tm, tf):
    t = x.shape[0]
    return pl.pallas_call(
        _mlp_kernel,
        grid=(t // tm, D_FF // tf),
        in_specs=[pl.BlockSpec((tm, D_MODEL), lambda i, f: (i, 0)),
                  pl.BlockSpec((1, D_MODEL), lambda i, f: (0, 0)),
                  pl.BlockSpec((D_MODEL, tf), lambda i, f: (0, f)),
                  pl.BlockSpec((tf, D_MODEL), lambda i, f: (f, 0))],
        out_specs=pl.BlockSpec((tm, D_MODEL), lambda i, f: (i, 0)),
        out_shape=jax.ShapeDtypeStruct((t, D_MODEL), F32),
        scratch_shapes=[pltpu.VMEM((tm, D_MODEL), BF16), pltpu.VMEM((tm, D_MODEL), F32)],
        compiler_params=_cparams(("parallel", "arbitrary")),
        name="sqrelu_mlp",
    )(x, lw["g_mlp"], lw["w_up"], lw["w_down"])


def _softmax_step(groups, m_ref, l_ref, acc_ref):
    stats = []
    for s, vals in groups:
        m_g = jnp.max(s, axis=1, keepdims=True)
        p = jnp.exp(s - m_g)
        stats.append((m_g, jnp.sum(p, axis=1, keepdims=True), _dot(p.astype(BF16), vals)))
    m_prev = m_ref[0][:, 0:1]
    m_new = m_prev
    for m_g, _, _ in stats:
        m_new = jnp.maximum(m_new, m_g)
    alpha = jnp.exp(m_prev - m_new)
    l_new = alpha * l_ref[0][:, 0:1]
    acc_new = alpha * acc_ref[0]
    for m_g, l_g, o_g in stats:
        w = jnp.exp(m_g - m_new)
        l_new = l_new + w * l_g
        acc_new = acc_new + w * o_g
    acc_ref[0] = acc_new
    m_ref[0] = jnp.broadcast_to(m_new, m_ref.shape[1:])
    l_ref[0] = jnp.broadcast_to(l_new, l_ref.shape[1:])


def _page_copy(src, buf, sem, layer, page_id, slot, i):
    return pltpu.make_async_copy(src.at[layer, page_id], buf.at[slot, i], sem.at[slot])


def _page_ring_step(pt_ref, srcs, bufs, sem, *, layer, pages):
    b, c = pl.program_id(0), pl.program_id(1)
    nc = pl.num_programs(1)
    step = b * nc + c
    slot = lax.rem(step, 2)

    def start(bb, cc, sl):
        for i in range(pages):
            pid = pt_ref[bb, cc * pages + i]
            for src, buf in zip(srcs, bufs):
                _page_copy(src, buf, sem, layer, pid, sl, i).start()

    @pl.when(step == 0)
    def _():
        start(b, c, slot)

    @pl.when(step + 1 < pl.num_programs(0) * nc)
    def _():
        wrap = c + 1 == nc
        start(jnp.where(wrap, b + 1, b), jnp.where(wrap, 0, c + 1), 1 - slot)

    for i in range(pages):
        for src, buf in zip(srcs, bufs):
            _page_copy(src, buf, sem, layer, 0, slot, i).wait()
    return slot


def _diff_paged_kernel(pt_ref, q_ref, kn_ref, vn_ref, ck_ref, cv_ref, o_ref, kbuf, vbuf, sem, m_s, l_s, acc,
                       *, pages, t_new, layer):
    c = pl.program_id(1)
    slot = _page_ring_step(pt_ref, (ck_ref, cv_ref), (kbuf, vbuf), sem, layer=layer, pages=pages)
    q = q_ref[...]
    rows = q.shape[0]

    @pl.when(c == 0)
    def _():
        pad = jnp.zeros((PAGE - t_new, 2 * LANES), F32)
        kn = jnp.concatenate([kn_ref[...], pad], axis=0).astype(BF16)
        vn = jnp.concatenate([vn_ref[...], pad], axis=0).astype(BF16)
        s = _dot_nt(q, kn)
        tq_ = lax.broadcasted_iota(jnp.int32, s.shape, 0) % t_new
        col = lax.broadcasted_iota(jnp.int32, s.shape, 1)
        s = jnp.where(col <= tq_, s, NEG)
        m = jnp.max(s, axis=1, keepdims=True)
        p = jnp.exp(s - m)
        m_s[0] = jnp.broadcast_to(m, (rows, LANES))
        l_s[0] = jnp.broadcast_to(jnp.sum(p, axis=1, keepdims=True), (rows, LANES))
        acc[0] = _dot(p.astype(BF16), vn)

    def v_page(i):
        return jnp.concatenate([vbuf[slot, i, pl.ds(j, PAGE, stride=2), :] for j in range(2)], axis=1).astype(BF16)

    groups = []
    for i in range(0, pages, GROUP_PAGES):
        kt = jnp.concatenate([kbuf[slot, i + j] for j in range(GROUP_PAGES)], axis=1).astype(BF16)
        vv = jnp.concatenate([v_page(i + j) for j in range(GROUP_PAGES)], axis=0)
        groups.append((_dot(q, kt), vv))
    _softmax_step(groups, m_s, l_s, acc)

    @pl.when(c == pl.num_programs(1) - 1)
    def _():
        o = acc[0] / l_s[0][:, 0:1]
        half = rows // 2
        o_ref[...] = jnp.concatenate([o[0:half, 0:LANES], o[half:rows, LANES:2 * LANES]], axis=0)


def _diff_paged_call(layer, pt, qbd, k_new, v_new, cache_k, cache_v, *, pages):
    b, n_pages = pt.shape
    rows = qbd.shape[1]
    t_new = k_new.shape[1]

    seq = lambda shp: pl.BlockSpec((None,) + shp, lambda b_, c, pt_: (b_, 0, 0))
    hbm = pl.BlockSpec(memory_space=pl.ANY)
    gs = pltpu.PrefetchScalarGridSpec(
        num_scalar_prefetch=1,
        grid=(b, n_pages // pages),
        in_specs=[seq((rows, 2 * LANES)), seq((t_new, 2 * LANES)), seq((t_new, 2 * LANES)), hbm, hbm],
        out_specs=seq((rows, LANES)),
        scratch_shapes=[pltpu.VMEM((2, pages, 2 * LANES, PAGE), F32), pltpu.VMEM((2, pages, 2 * PAGE, DIFF_VD), F32),
                        pltpu.SemaphoreType.DMA((2,)),
                        pltpu.VMEM((1, rows, LANES), F32), pltpu.VMEM((1, rows, LANES), F32),
                        pltpu.VMEM((1, rows, 2 * LANES), F32)],
    )
    return pl.pallas_call(
        functools.partial(_diff_paged_kernel, pages=pages, t_new=t_new, layer=layer),
        grid_spec=gs,
        out_shape=jax.ShapeDtypeStruct((b, rows, LANES), F32),
        compiler_params=_cparams(("arbitrary", "arbitrary")),
        name="diff_paged_attention",
    )(pt, qbd, k_new, v_new, cache_k, cache_v)


def _mla_absorb_kernel(qm_ref, wabs_ref, lq_ref):
    for h in range(MLA_HEADS):
        lq_ref[:, h * 2 * LANES:(h + 1) * 2 * LANES] = _dot(
            qm_ref[:, h * LANES:(h + 1) * LANES], wabs_ref[h]).astype(BF16)


def _mla_absorb_call(qm, wabs):
    t = qm.shape[0]
    tm = min(t, 256)
    return pl.pallas_call(
        _mla_absorb_kernel,
        grid=(t // tm,),
        in_specs=[pl.BlockSpec((tm, MLA_HEADS * LANES), lambda i: (i, 0)),
                  pl.BlockSpec(wabs.shape, lambda i: (0, 0, 0))],
        out_specs=pl.BlockSpec((tm, MLA_HEADS * 2 * LANES), lambda i: (i, 0)),
        out_shape=jax.ShapeDtypeStruct((t, MLA_HEADS * 2 * LANES), BF16),
        compiler_params=_cparams(("parallel",)),
        name="mla_absorb_q",
    )(qm, wabs)


def _mla_paged_kernel(pt_ref, lq_ref, lr_ref, wrow_ref, cn_ref, rn_ref, tcn_ref, tsn_ref, tc_ref, ts_ref,
                      gr_ref, wuv_ref, cc_ref, cr_ref, o_ref, cbuf, rbuf, sem, m_s, l_s, acc, *, pages, t_new, layer):
    c = pl.program_id(1)
    slot = _page_ring_step(pt_ref, (cc_ref, cr_ref), (cbuf, rbuf), sem, layer=layer, pages=pages)
    l_lat = jnp.concatenate([lq_ref[...], wrow_ref[...]], axis=0)
    l_rope = lr_ref[...]
    nq = lq_ref.shape[0]
    gr = gr_ref[...]
    half = MLA_ROPE // 2

    def scores(ckv_b, kr_t, tcos, tsin):
        x = kr_t * gr
        x1, x2 = x[0:half], x[half:MLA_ROPE]
        roped = jnp.concatenate([x1 * tcos - x2 * tsin, x2 * tcos + x1 * tsin], axis=0).astype(BF16)
        ss_rope = jnp.sum(kr_t * kr_t, axis=0, keepdims=True)
        out = _dot_nt(l_lat, ckv_b)
        raw = out[0:nq] + _dot(l_rope, roped)
        parts = []
        for h in range(MLA_HEADS):
            kn = out[nq + h * MLA_NOPE:nq + (h + 1) * MLA_NOPE, :]
            ss = jnp.sum(kn * kn, axis=0, keepdims=True) + ss_rope
            parts.append(raw[h * t_new:(h + 1) * t_new, :] * lax.rsqrt(ss * (1.0 / MLA_QK) + EPS))
        return jnp.concatenate(parts, axis=0)

    @pl.when(c == 0)
    def _():
        padc = jnp.zeros((PAGE - t_new, KV_LORA), F32)
        cb = jnp.concatenate([cn_ref[...], padc], axis=0).astype(BF16)
        s = scores(cb, rn_ref[...], tcn_ref[...], tsn_ref[...])
        tq_ = lax.broadcasted_iota(jnp.int32, s.shape, 0) % t_new
        col = lax.broadcasted_iota(jnp.int32, s.shape, 1)
        s = jnp.where(col <= tq_, s, NEG)
        m = jnp.max(s, axis=1, keepdims=True)
        p = jnp.exp(s - m)
        m_s[0] = jnp.broadcast_to(m, (nq, LANES))
        l_s[0] = jnp.broadcast_to(jnp.sum(p, axis=1, keepdims=True), (nq, LANES))
        acc[0] = _dot(p.astype(BF16), cb)

    groups = []
    gw = GROUP_PAGES * PAGE
    for i in range(0, pages, GROUP_PAGES):
        cb = jnp.concatenate([cbuf[slot, i + j] for j in range(GROUP_PAGES)], axis=0).astype(BF16)
        kr_t = jnp.concatenate([rbuf[slot, i + j] for j in range(GROUP_PAGES)], axis=1)
        off = pl.multiple_of((c * pages + i) * PAGE, gw)
        groups.append((scores(cb, kr_t, tc_ref[:, pl.ds(off, gw)], ts_ref[:, pl.ds(off, gw)]), cb))
    _softmax_step(groups, m_s, l_s, acc)

    @pl.when(c == pl.num_programs(1) - 1)
    def _():
        o = (acc[0] / l_s[0][:, 0:1]).astype(BF16)
        o_ref[...] = jnp.concatenate(
            [_dot(o[h * t_new:(h + 1) * t_new, :], wuv_ref[h]) for h in range(MLA_HEADS)], axis=0)


def _mla_paged_call(layer, pt, lq, ckv_new, kr_new, cache_ckv, cache_kr, lw, ptabs, *, pages):
    b, n_pages = pt.shape
    nq = lq.shape[1]
    t_new = ckv_new.shape[1]
    lq_lat = lq[:, :, 0:KV_LORA]
    lq_rope = lq[:, :, KV_LORA:KV_LORA + MLA_ROPE]

    seq = lambda shp: pl.BlockSpec((None,) + shp, lambda b_, c, pt_: (b_, 0, 0))
    full = lambda a: pl.BlockSpec(a.shape, lambda b_, c, pt_: (0,) * a.ndim)
    hbm = pl.BlockSpec(memory_space=pl.ANY)
    consts = [ptabs["cn"], ptabs["sn"], ptabs["c"], ptabs["s"], lw["g_kr"], lw["w_uvh"]]
    gs = pltpu.PrefetchScalarGridSpec(
        num_scalar_prefetch=1,
        grid=(b, n_pages // pages),
        in_specs=[seq((nq, KV_LORA)), seq((nq, MLA_ROPE)), full(lw["w_rows"]), seq((t_new, KV_LORA)),
                  seq((MLA_ROPE, PAGE))] + [full(a) for a in consts] + [hbm, hbm],
        out_specs=seq((nq, LANES)),
        scratch_shapes=[pltpu.VMEM((2, pages, PAGE, KV_LORA), F32), pltpu.VMEM((2, pages, MLA_ROPE, PAGE), F32),
                        pltpu.SemaphoreType.DMA((2,)),
                        pltpu.VMEM((1, nq, LANES), F32), pltpu.VMEM((1, nq, LANES), F32),
                        pltpu.VMEM((1, nq, LANES), F32)],
    )
    return pl.pallas_call(
        functools.partial(_mla_paged_kernel, pages=pages, t_new=t_new, layer=layer),
        grid_spec=gs,
        out_shape=jax.ShapeDtypeStruct((b, nq, LANES), F32),
        compiler_params=_cparams(("arbitrary", "arbitrary")),
        name="mla_paged_attention",
    )(pt, lq_lat, lq_rope, lw["w_rows"], ckv_new, kr_new, *consts, cache_ckv, cache_kr)


def _block_ones(n, seg):
    i = jnp.arange(n) // seg
    return (i[:, None] == i[None, :]).astype(BF16)


def _pad_heads(w, used):
    lead = w.shape[:-1]
    w = w.reshape(lead + (MLA_HEADS, used))
    w = jnp.pad(w, [(0, 0)] * len(lead) + [(0, 0), (0, LANES - used)])
    return w.reshape(lead + (MLA_HEADS * LANES,))


def _layer_weights(p, l):
    row = lambda a: a[l][None, :].astype(F32)
    w_ukv = p["w_ukv"][l].reshape(KV_LORA, MLA_HEADS, MLA_NOPE + MLA_VD)
    w_uk = w_ukv[:, :, :MLA_NOPE]
    w_uv = w_ukv[:, :, MLA_NOPE:]
    g_mk = p["mla_k_norm"][l]
    pad_g = lambda g: jnp.tile(jnp.pad(g, (0, LANES - MLA_QK)), MLA_HEADS)[None, :]
    wabs = jnp.zeros((MLA_HEADS, LANES, 2 * LANES), F32)
    wabs = wabs.at[:, :MLA_NOPE, :KV_LORA].set(jnp.transpose(w_uk, (1, 2, 0)) * g_mk[None, :MLA_NOPE, None])
    wabs = wabs.at[:, MLA_NOPE + jnp.arange(MLA_ROPE), KV_LORA + jnp.arange(MLA_ROPE)].set(1.0)
    w_rows = jnp.transpose(w_uk, (1, 2, 0)).reshape(MLA_HEADS * MLA_NOPE, KV_LORA)
    return {
        "g_attn": row(p["norm_attn"]),
        "w_in": jnp.pad(p["w_in"][l], ((0, 0), (0, IN_COLS_PAD - IN_COLS))).astype(BF16),
        "g_dq": jnp.tile(p["diff_q_norm"][l], 8)[None, :],
        "g_dk": jnp.tile(p["diff_k_norm"][l], 4)[None, :],
        "b64": _block_ones(512, 64),
        "b128": _block_ones(512, 128),
        "b64c": _block_ones(256, 64),
        "g_ql": row(p["mla_q_lora_norm"]),
        "w_uq": _pad_heads(p["w_uq"][l], MLA_QK).astype(BF16),
        "g_mq": pad_g(p["mla_q_norm"][l]),
        "g_kvl": row(p["mla_kv_lora_norm"]),
        "w_uk": _pad_heads(w_uk.reshape(KV_LORA, MLA_HEADS * MLA_NOPE), MLA_NOPE).astype(BF16),
        "w_uv": w_uv.reshape(KV_LORA, MLA_HEADS * MLA_VD).astype(BF16),
        "g_mk": pad_g(g_mk),
        "lam": p["diff_lambda"][l],
        "g_sub": row(p["diff_subln"]),
        "g_mo": row(p["mla_out_norm"]),
        "w_o": p["w_o"][l].astype(BF16),
        "g_mem": row(p["norm_mem"]),
        "w_ck": p["w_ck"][l].astype(BF16),
        "w_cv": p["w_cv"][l].astype(BF16),
        "g_ck": jnp.tile(p["cross_k_norm"][l], 4)[None, :],
        "g_cross": row(p["norm_cross"]),
        "w_cq": p["w_cq"][l].astype(BF16),
        "g_cq": jnp.tile(p["cross_q_norm"][l], 4)[None, :],
        "w_co": p["w_co"][l].astype(BF16),
        "g_mlp": row(p["norm_mlp"]),
        "w_up": p["w_up"][l].astype(BF16),
        "w_down": p["w_down"][l].astype(BF16),
        "wabs": wabs.astype(BF16),
        "w_rows": w_rows.astype(BF16),
        "g_kr": g_mk[MLA_NOPE:, None],
        "w_uvh": jnp.transpose(w_uv, (1, 0, 2)).astype(BF16),
    }


def _angles(pos, d, theta):
    inv = theta ** (-jnp.arange(0, d, 2, dtype=F32) / d)
    ang = pos.astype(F32)[:, None] * inv[None, :]
    return jnp.cos(ang), jnp.sin(ang)


def _mixer_tables(pos):
    n = pos.shape[0]
    one = lambda w: jnp.ones((n, w), F32)
    zero = lambda w: jnp.zeros((n, w), F32)
    cos, sin = _angles(pos, DIFF_ROT, ROPE_THETA)
    rest = DIFF_HD - DIFF_ROT
    dc = jnp.concatenate([cos, cos, one(rest)] * 2, axis=1)
    ds1 = jnp.concatenate([-sin, zero(DIFF_HD - 8)] * 2, axis=1)
    ds2 = jnp.concatenate([zero(8), sin, zero(rest)] * 2, axis=1)
    cos, sin = _angles(pos, MLA_ROPE, MLA_THETA)
    mc = jnp.concatenate([one(MLA_NOPE), cos, cos, one(LANES - MLA_QK)], axis=1)
    ms1 = jnp.concatenate([zero(MLA_NOPE), -sin, zero(LANES - MLA_NOPE - 16)], axis=1)
    ms2 = jnp.concatenate([zero(MLA_NOPE + 16), sin, zero(LANES - MLA_QK)], axis=1)
    return {"dc": dc, "ds1": ds1, "ds2": ds2, "mc": mc, "ms1": ms1, "ms2": ms2}


def _paged_tables(pos):
    cos, sin = _angles(pos, MLA_ROPE, MLA_THETA)
    return cos.T, sin.T


def _pick(n, pref):
    while n % pref:
        pref //= 2
    return pref


def kernel(x_prompt, x_sample, cache_diff_k, cache_diff_v, cache_mla_ckv, cache_mla_krope, cache_mem_k, cache_mem_v, page_table, mem_prompt, norm_attn, w_in, diff_q_norm, diff_k_norm, diff_lambda, diff_subln, mla_q_lora_norm, w_uq, mla_kv_lora_norm, w_ukv, mla_q_norm, mla_k_norm, mla_out_norm, w_o, norm_cross, norm_mem, w_cq, w_ck, w_cv, cross_q_norm, cross_k_norm, w_co, norm_mlp, w_up, w_down):
    params = dict(norm_attn=norm_attn, w_in=w_in, diff_q_norm=diff_q_norm, diff_k_norm=diff_k_norm,
                  diff_lambda=diff_lambda, diff_subln=diff_subln, mla_q_lora_norm=mla_q_lora_norm, w_uq=w_uq,
                  mla_kv_lora_norm=mla_kv_lora_norm, w_ukv=w_ukv, mla_q_norm=mla_q_norm, mla_k_norm=mla_k_norm,
                  mla_out_norm=mla_out_norm, w_o=w_o, norm_cross=norm_cross, norm_mem=norm_mem, w_cq=w_cq,
                  w_ck=w_ck, w_cv=w_cv, cross_q_norm=cross_q_norm, cross_k_norm=cross_k_norm, w_co=w_co,
                  norm_mlp=norm_mlp, w_up=w_up, w_down=w_down)
    depth = w_in.shape[0]
    bp, sp, _ = x_prompt.shape
    db, ts, _ = x_sample.shape
    n_pool = cache_diff_k.shape[1]
    n_pages = page_table.shape[1]
    mem_len = mem_prompt.shape[1]
    tp, tsm = bp * sp, db * ts

    tm_p = _pick(sp, 256)
    tm_s = _pick(tsm, 256)
    ptab = _mixer_tables(jnp.arange(sp))
    spos = n_pages * PAGE + jnp.arange(ts)
    stab = {k: jnp.tile(v, (tm_s // ts, 1)) for k, v in _mixer_tables(spos).items()}
    pages = _pick(n_pages, 16)
    pages_mla = _pick(n_pages, 32)
    pc, ps = _paged_tables(jnp.arange(n_pages * PAGE))
    pcn, psn = _paged_tables(spos)
    padn = lambda a: jnp.pad(a, ((0, 0), (0, PAGE - ts)))
    paged_tabs = {"c": pc, "s": ps, "cn": padn(pcn), "sn": padn(psn)}

    ck = jnp.transpose(cache_diff_k, (0, 1, 3, 4, 5, 2)).reshape(depth, n_pool, 2 * LANES, PAGE)
    cv = cache_diff_v.reshape(depth, n_pool, 2 * PAGE, DIFF_VD)
    ckr = jnp.transpose(cache_mla_krope, (0, 1, 3, 2))

    xp = x_prompt.reshape(tp, D_MODEL)
    xs = x_sample.reshape(tsm, D_MODEL)
    mem = mem_prompt.reshape(bp * mem_len, D_MODEL)
    tq_d, tq_m, tk = _pick(sp, 512), _pick(sp, 1024), _pick(sp, 512)
    tm_c = _pick(sp, 512)
    nseq_s = _pick(db, 16)
    tm_mlp_p, tm_mlp_s = _pick(tp, 1024), _pick(tsm, 1024)

    p_rows, s_rows = [], []
    for l in range(depth):
        lw = _layer_weights(params, l)

        qd, dk, dv, kdb, vdb, qm, ckv, kr, km, vm = _mixer_call(xp, lw, ptab, sp // tm_p, tm_p)
        od = _flash_call(qd.reshape(bp, sp, 1024), kdb.reshape(bp, sp, 256), vdb.reshape(bp, sp, 256),
                         groups=4, tq=tq_d, tk=tk, name="diff_flash")
        om = _flash_call(qm.reshape(bp, sp, 512), km.reshape(bp, sp, 512), vm.reshape(bp, sp, 512),
                         groups=1, tq=tq_m, tk=tk, name="mla_flash")
        xp = _merge_call(od.reshape(tp, 1024), om.reshape(tp, 512), xp, lw, l, tm_p)
        mk, mv = _memkv_call(mem, lw)
        steps_per_batch = sp // tm_c
        xp = _cross_call(xp, mk.reshape(bp, mem_len, CROSS_WIDTH), mv.reshape(bp, mem_len, CROSS_WIDTH), lw,
                         nseq=1, tpm=tm_c, mem_map=lambda i: i // steps_per_batch)
        xp = _mlp_call(xp, lw, tm_mlp_p, 1024)
        p_rows.append((dk, dv, ckv, kr, mk, mv))

        qd, dk, dv, _, _, qm, ckv, kr, _, _ = _mixer_call(xs, lw, stab, 1, tm_s)
        qh = jnp.transpose(qd.reshape(db, ts, 2, 4, LANES), (0, 2, 3, 1, 4)).reshape(db, 2, 4 * ts, LANES)
        zq = jnp.zeros_like(qh[:, 0])
        qbd = jnp.concatenate([jnp.concatenate([qh[:, 0], zq], axis=-1),
                               jnp.concatenate([zq, qh[:, 1]], axis=-1)], axis=1)
        od = _diff_paged_call(l, page_table, qbd, dk.reshape(db, ts, 256), dv.reshape(db, ts, 256), ck, cv,
                              pages=pages)
        od = jnp.transpose(od.reshape(db, 8, ts, LANES), (0, 2, 1, 3)).reshape(tsm, 1024)
        lq = _mla_absorb_call(qm, lw["wabs"])
        lq = jnp.transpose(lq.reshape(db, ts, MLA_HEADS, 2 * LANES), (0, 2, 1, 3)).reshape(db, MLA_HEADS * ts, 2 * LANES)
        kr_t = jnp.pad(jnp.transpose(kr.reshape(db, ts, MLA_ROPE), (0, 2, 1)), ((0, 0), (0, 0), (0, PAGE - ts)))
        om = _mla_paged_call(l, page_table, lq, ckv.reshape(db, ts, KV_LORA), kr_t,
                             cache_mla_ckv, ckr, lw, paged_tabs, pages=pages_mla)
        om = jnp.transpose(om.reshape(db, MLA_HEADS, ts, LANES), (0, 2, 1, 3)).reshape(tsm, 512)
        xs = _merge_call(od, om, xs, lw, l, tm_s)
        xs = _cross_call(xs, cache_mem_k[l].reshape(db, mem_len, CROSS_WIDTH),
                         cache_mem_v[l].reshape(db, mem_len, CROSS_WIDTH), lw,
                         nseq=nseq_s, tpm=ts, mem_map=lambda i: i)
        xs = _mlp_call(xs, lw, tm_mlp_s, 1024)
        s_rows.append((dk, dv, ckv, kr))

    st = lambda rows, i, shp: jnp.stack([r[i] for r in rows]).reshape((depth,) + shp)
    return (xp.reshape(bp, sp, D_MODEL), xs.reshape(db, ts, D_MODEL),
            st(p_rows, 0, (bp, sp, DIFF_KV_HEADS, 2, DIFF_HD)), st(p_rows, 1, (bp, sp, DIFF_KV_HEADS, DIFF_VD)),
            st(p_rows, 2, (bp, sp, KV_LORA)), st(p_rows, 3, (bp, sp, MLA_ROPE)),
            st(p_rows, 4, (bp, mem_len, CROSS_HEADS, CROSS_HD)), st(p_rows, 5, (bp, mem_len, CROSS_HEADS, CROSS_HD)),
            st(s_rows, 0, (db, ts, DIFF_KV_HEADS, 2, DIFF_HD)), st(s_rows, 1, (db, ts, DIFF_KV_HEADS, DIFF_VD)),
            st(s_rows, 2, (db, ts, KV_LORA)), st(s_rows, 3, (db, ts, MLA_ROPE)))
```
